```python
import math
import jax, jax.numpy as jnp
from jax import lax
import numpy as np

D_MODEL = 2048
BATCH = 2
SEQ = 8192
DEPTH = 4

N_A_LAYERS = DEPTH // 2
N_B_LAYERS = DEPTH - N_A_LAYERS
HEAD_DIM = 64
N_HEADS = D_MODEL // HEAD_DIM
N_KV_A = N_HEADS // 4
N_KV_B = N_HEADS // 8
ATTN_WIDTH = N_HEADS * HEAD_DIM
ROPE_DIM = HEAD_DIM // 4
ROPE_THETA = 500000.0
MOBA_BLOCK = 256
MOBA_TOPK = 3
MOBA_QCHUNK = 32
SWA_WINDOW = 128
N_EXPERTS = 32
TOP_K = 4
D_EXPERT = D_MODEL // 2
SWIGLU_LIMIT = 7.0
SWIGLU_ALPHA = 1.702
MOE_ROW_BLOCK = 256
LN_EPS = 1e-5
DEEPNORM_ALPHA = (2 * DEPTH) ** 0.25
DEEPNORM_BETA = (8 * DEPTH) ** -0.25

kernel_name = "yoco_moba_swa_sink_moe_deepnorm"


def layer_norm(x, g, b):
    xf = x.astype(jnp.float32)
    mu = xf.mean(-1, keepdims=True)
    var = jnp.square(xf - mu).mean(-1, keepdims=True)
    y = (xf - mu) * lax.rsqrt(var + LN_EPS) * g.astype(jnp.float32) + b.astype(jnp.float32)
    return y.astype(x.dtype)


def rope_tables(seq):
    inv = ROPE_THETA ** (-jnp.arange(0, ROPE_DIM, 2, dtype=jnp.float32) / ROPE_DIM)
    ang = jnp.arange(seq, dtype=jnp.float32)[:, None] * inv[None, :]
    return jnp.cos(ang), jnp.sin(ang)


def partial_rope(x, cos, sin):
    half = ROPE_DIM // 2
    x1 = x[..., :half].astype(jnp.float32)
    x2 = x[..., half:ROPE_DIM].astype(jnp.float32)
    c = cos[:, None, :]
    s = sin[:, None, :]
    rot = jnp.concatenate([x1 * c - x2 * s, x2 * c + x1 * s], axis=-1).astype(x.dtype)
    return jnp.concatenate([rot, x[..., ROPE_DIM:]], axis=-1)


def moba_attention(q, k, v):
    B, S, H, D = q.shape
    Hkv = k.shape[2]
    G = H // Hkv
    nb = -(-S // MOBA_BLOCK)
    s_pad = nb * MOBA_BLOCK
    pad = ((0, 0), (0, s_pad - S), (0, 0), (0, 0))
    q, k, v = jnp.pad(q, pad), jnp.pad(k, pad), jnp.pad(v, pad)
    scale = D ** -0.5
    qh = q.reshape(B, s_pad, Hkv, G, D).transpose(0, 2, 3, 1, 4)
    kb = k.transpose(0, 2, 1, 3).reshape(B, Hkv, nb, MOBA_BLOCK, D)
    vb = v.transpose(0, 2, 1, 3).reshape(B, Hkv, nb, MOBA_BLOCK, D)
    kmean = kb.astype(jnp.float32).mean(axis=3)
    gate = jnp.einsum('bkgsd,bknd->bkgsn', qh.astype(jnp.float32), kmean)
    q_blk = jnp.arange(s_pad) // MOBA_BLOCK
    fully_past = jnp.arange(nb)[None, :] < q_blk[:, None]
    gate = jnp.where(fully_past, gate, -jnp.inf)
    n_sel = min(MOBA_TOPK, nb)
    _, sel = lax.top_k(gate, n_sel)

    nc = s_pad // MOBA_QCHUNK
    q_c = qh.reshape(B, Hkv, G, nc, MOBA_QCHUNK, D).transpose(3, 0, 1, 2, 4, 5)
    sel_c = sel.reshape(B, Hkv, G, nc, MOBA_QCHUNK, n_sel).transpose(3, 0, 1, 2, 4, 5)
    bi = jnp.arange(B)[:, None, None, None, None]
    hi = jnp.arange(Hkv)[None, :, None, None, None]

    def chunk(args):
        qc, selc, c = args
        t = c * MOBA_QCHUNK + jnp.arange(MOBA_QCHUNK)
        blk = (c * MOBA_QCHUNK) // MOBA_BLOCK
        k_own = lax.dynamic_index_in_dim(kb, blk, axis=2, keepdims=False)
        v_own = lax.dynamic_index_in_dim(vb, blk, axis=2, keepdims=False)
        s_own = jnp.einsum('bkgqd,bksd->bkgqs', qc, k_own).astype(jnp.float32) * scale
        key_pos = blk * MOBA_BLOCK + jnp.arange(MOBA_BLOCK)
        s_own = jnp.where(key_pos[None, :] <= t[:, None], s_own, -jnp.inf)
        k_sel = kb[bi, hi, selc]
        v_sel = vb[bi, hi, selc]
        s_past = jnp.einsum('bkgqd,bkgqnsd->bkgqns', qc, k_sel).astype(jnp.float32) * scale
        valid = jnp.arange(n_sel) < blk
        s_past = jnp.where(valid[:, None], s_past, -jnp.inf)
        s_past = s_past.reshape(B, Hkv, G, MOBA_QCHUNK, n_sel * MOBA_BLOCK)
        p = jax.nn.softmax(jnp.concatenate([s_own, s_past], axis=-1), axis=-1).astype(v.dtype)
        p_own = p[..., :MOBA_BLOCK]
        p_past = p[..., MOBA_BLOCK:].reshape(B, Hkv, G, MOBA_QCHUNK, n_sel, MOBA_BLOCK)
        return (jnp.einsum('bkgqs,bksd->bkgqd', p_own, v_own)
                + jnp.einsum('bkgqns,bkgqnsd->bkgqd', p_past, v_sel))

    o = lax.map(chunk, (q_c, sel_c, jnp.arange(nc)))
    o = o.transpose(1, 0, 4, 2, 3, 5).reshape(B, s_pad, H * D)
    return o[:, :S]


def swa_sink_attention(q, k, v, sinks):
    B, S, H, D = q.shape
    Hkv = k.shape[2]
    G = H // Hkv
    W = SWA_WINDOW
    nb = S // W
    scale = D ** -0.5
    qb = q.reshape(B, nb, W, Hkv, G, D)
    kb = k.reshape(B, nb, W, Hkv, D)
    vb = v.reshape(B, nb, W, Hkv, D)
    shift = ((0, 0), (1, 0), (0, 0), (0, 0), (0, 0))
    kk = jnp.concatenate([jnp.pad(kb, shift)[:, :-1], kb], axis=2)
    vv = jnp.concatenate([jnp.pad(vb, shift)[:, :-1], vb], axis=2)
    s = jnp.einsum('bnqkgd,bnskd->bkgnqs', qb, kk).astype(jnp.float32) * scale
    qi = jnp.arange(W)[:, None]
    si = jnp.arange(2 * W)[None, :]
    diff = qi + W - si
    band = (diff >= 0) & (diff < W)
    mask = band[None] & ((jnp.arange(nb) > 0)[:, None, None] | (si >= W)[None])
    s = jnp.where(mask, s, -jnp.inf)
    sink = sinks.astype(jnp.float32).reshape(Hkv, G)[None, :, :, None, None, None]
    m = jnp.maximum(s.max(-1, keepdims=True), sink)
    p = jnp.exp(s - m)
    p = (p / (p.sum(-1, keepdims=True) + jnp.exp(sink - m))).astype(v.dtype)
    o = jnp.einsum('bkgnqs,bnskd->bnqkgd', p, vv)
    return o.reshape(B, S, H * D)


def clamped_swiglu(h):
    glu, lin = h[..., ::2], h[..., 1::2]
    glu = jnp.minimum(glu, SWIGLU_LIMIT)
    lin = jnp.clip(lin, -SWIGLU_LIMIT, SWIGLU_LIMIT)
    return glu * jax.nn.sigmoid(SWIGLU_ALPHA * glu) * (lin + 1.0)


def moe(x2d, w_r, b_r, w_gu, b_gu, w_dn, b_dn):
    N, D = x2d.shape
    logits = (x2d @ w_r + b_r).astype(jnp.float32)
    top_val, top_idx = lax.top_k(logits, TOP_K)
    gates = jax.nn.softmax(top_val, axis=-1)
    nk = N * TOP_K
    e_flat = top_idx.reshape(-1)
    tok_flat = jnp.repeat(jnp.arange(N, dtype=jnp.int32), TOP_K)
    g_flat = gates.reshape(-1)
    order = jnp.argsort(e_flat)
    es, ts, gs = e_flat[order], tok_flat[order], g_flat[order]
    counts = jnp.bincount(e_flat, length=N_EXPERTS)
    start = jnp.cumsum(counts) - counts
    padded = ((counts + MOE_ROW_BLOCK - 1) // MOE_ROW_BLOCK) * MOE_ROW_BLOCK
    pstart = jnp.cumsum(padded) - padded
    pend = pstart + padded
    dest = pstart[es] + (jnp.arange(nk) - start[es])
    n_rows = (-(-nk // MOE_ROW_BLOCK) + N_EXPERTS) * MOE_ROW_BLOCK
    n_blocks = n_rows // MOE_ROW_BLOCK
    row_tok = jnp.full((n_rows,), N, jnp.int32).at[dest].set(ts)
    row_gate = jnp.zeros((n_rows,), jnp.float32).at[dest].set(gs)
    block_expert = jnp.minimum(
        jnp.searchsorted(pend, jnp.arange(n_blocks) * MOE_ROW_BLOCK, side='right'), N_EXPERTS - 1)
    x_pad = jnp.concatenate([x2d, jnp.zeros((1, D), x2d.dtype)], axis=0)
    xr = x_pad[row_tok].reshape(n_blocks, MOE_ROW_BLOCK, D)

    def expert_block(args):
        xb, e = args
        h = clamped_swiglu(xb @ w_gu[e] + b_gu[e])
        return h @ w_dn[e] + b_dn[e]

    yr = lax.map(expert_block, (xr, block_expert)).reshape(n_rows, D)
    y = jax.ops.segment_sum(yr * row_gate[:, None].astype(yr.dtype), row_tok, num_segments=N + 1)
    return y[:N]


def setup_inputs(seed: int = 0) -> dict:
    key = jax.random.key(seed)
    ks = jax.random.split(key, 20)
    f32 = jnp.float32
    qkv_a = ATTN_WIDTH + 2 * N_KV_A * HEAD_DIM
    kv_b = 2 * N_KV_B * HEAD_DIM

    def nrm(k, shape, scale):
        return jax.random.normal(k, shape, f32) * scale

    return {
        "x": nrm(ks[0], (BATCH, SEQ, D_MODEL), 1.0),
        "w_qkv_a": nrm(ks[1], (N_A_LAYERS, D_MODEL, qkv_a), D_MODEL ** -0.5),
        "w_o_a": nrm(ks[2], (N_A_LAYERS, ATTN_WIDTH, D_MODEL), ATTN_WIDTH ** -0.5 * DEEPNORM_BETA),
        "w_q_b": nrm(ks[3], (N_B_LAYERS, D_MODEL, ATTN_WIDTH), D_MODEL ** -0.5),
        "b_q_b": nrm(ks[4], (N_B_LAYERS, ATTN_WIDTH), 0.02),
        "sinks_b": nrm(ks[5], (N_B_LAYERS, N_HEADS), 0.5),
        "w_o_b": nrm(ks[6], (N_B_LAYERS, ATTN_WIDTH, D_MODEL), ATTN_WIDTH ** -0.5 * DEEPNORM_BETA),
        "b_o_b": nrm(ks[7], (N_B_LAYERS, D_MODEL), 0.02),
        "w_kv_shared": nrm(ks[8], (D_MODEL, kv_b), D_MODEL ** -0.5),
        "b_kv_shared": nrm(ks[9], (kv_b,), 0.02),
        "ln_mix_g": 1.0 + nrm(ks[10], (DEPTH, D_MODEL), 0.05),
        "ln_mix_b": nrm(ks[11], (DEPTH, D_MODEL), 0.02),
        "ln_ffn_g": 1.0 + nrm(ks[12], (DEPTH, D_MODEL), 0.05),
        "ln_ffn_b": nrm(ks[13], (DEPTH, D_MODEL), 0.02),
        "w_router": nrm(ks[14], (DEPTH, D_MODEL, N_EXPERTS), D_MODEL ** -0.5),
        "b_router": nrm(ks[15], (DEPTH, N_EXPERTS), 0.01),
        "w_gate_up": nrm(ks[16], (DEPTH, N_EXPERTS, D_MODEL, 2 * D_EXPERT), D_MODEL ** -0.5),
        "b_gate_up": nrm(ks[17], (DEPTH, N_EXPERTS, 2 * D_EXPERT), 0.02),
        "w_down": nrm(ks[18], (DEPTH, N_EXPERTS, D_EXPERT, D_MODEL), D_EXPERT ** -0.5 * DEEPNORM_BETA),
        "b_down": nrm(ks[19], (DEPTH, N_EXPERTS, D_MODEL), 0.02),
    }


def reference(x, w_qkv_a, w_o_a, w_q_b, b_q_b, sinks_b, w_o_b, b_o_b, w_kv_shared, b_kv_shared,
              ln_mix_g, ln_mix_b, ln_ffn_g, ln_ffn_b, w_router, b_router, w_gate_up, b_gate_up,
              w_down, b_down):
    B, S, D = x.shape
    cos, sin = rope_tables(S)
    k_shared = None
    v_shared = None
    for layer in range(DEPTH):
        if layer < N_A_LAYERS:
            qkv = x @ w_qkv_a[layer]
            q = qkv[..., :ATTN_WIDTH].reshape(B, S, N_HEADS, HEAD_DIM)
            k = qkv[..., ATTN_WIDTH:ATTN_WIDTH + N_KV_A * HEAD_DIM].reshape(B, S, N_KV_A, HEAD_DIM)
            v = qkv[..., ATTN_WIDTH + N_KV_A * HEAD_DIM:].reshape(B, S, N_KV_A, HEAD_DIM)
            q, k = partial_rope(q, cos, sin), partial_rope(k, cos, sin)
            mix = moba_attention(q, k, v) @ w_o_a[layer]
        else:
            j = layer - N_A_LAYERS
            if j == 0:
                kv = x @ w_kv_shared + b_kv_shared
                k_shared = partial_rope(
                    kv[..., :N_KV_B * HEAD_DIM].reshape(B, S, N_KV_B, HEAD_DIM), cos, sin)
                v_shared = kv[..., N_KV_B * HEAD_DIM:].reshape(B, S, N_KV_B, HEAD_DIM)
            q = partial_rope((x @ w_q_b[j] + b_q_b[j]).reshape(B, S, N_HEADS, HEAD_DIM), cos, sin)
            mix = swa_sink_attention(q, k_shared, v_shared, sinks_b[j]) @ w_o_b[j] + b_o_b[j]
        x = layer_norm(DEEPNORM_ALPHA * x + mix, ln_mix_g[layer], ln_mix_b[layer])
        y = moe(x.reshape(B * S, D), w_router[layer], b_router[layer], w_gate_up[layer],
                b_gate_up[layer], w_down[layer], b_down[layer]).reshape(B, S, D)
        x = layer_norm(DEEPNORM_ALPHA * x + y, ln_ffn_g[layer], ln_ffn_b[layer])
    return x
```

```python
import functools

import numpy as np
import jax
import jax.numpy as jnp
from jax import lax
from jax.experimental import pallas as pl
from jax.experimental.pallas import tpu as pltpu

F32 = jnp.float32
BF16 = jnp.bfloat16
NEG_INF = float("-inf")

HEAD_DIM = 64
N_HEADS = 32
N_KV_A = 8
N_KV_B = 4
ROPE_DIM = 16
ROPE_THETA = 500000.0
MOBA_BLOCK = 256
MOBA_TOPK = 3
SWA_WINDOW = 128
N_EXPERTS = 32
TOP_K = 4
SWIGLU_LIMIT = 7.0
SWIGLU_ALPHA = 1.702
LN_EPS = 1e-5

LANES = 128
V7X_VMEM_BYTES = 64 * 1024 * 1024
VMEM_LIMIT = V7X_VMEM_BYTES - 8 * 1024 * 1024

TM_PROJ = 512
TM_OPROJ = 256
TM_MOE = 512
TM_COMBINE = 256
HC_MOE = 512


def _cparams(semantics):
    return pltpu.CompilerParams(dimension_semantics=semantics, vmem_limit_bytes=VMEM_LIMIT)


def _proj_kernel(x_ref, w_ref, b_ref, c_ref, sa_ref, sb_ref, *o_refs, splits, rope_cols, tn):
    x = x_ref[...].astype(BF16)
    col = 0
    for o_ref, width in zip(o_refs, splits):
        for c0 in range(0, width, tn):
            lo = col + c0
            y = jnp.dot(x, w_ref[:, lo:lo + tn], preferred_element_type=F32) + b_ref[:, lo:lo + tn]
            if lo < rope_cols:
                for l0 in range(0, tn, LANES):
                    seg = y[:, l0:l0 + LANES]
                    rot = (seg * c_ref[...] + pltpu.roll(seg, LANES - ROPE_DIM // 2, 1) * sa_ref[...]
                           + pltpu.roll(seg, ROPE_DIM // 2, 1) * sb_ref[...])
                    o_ref[:, c0 + l0:c0 + l0 + LANES] = rot.astype(o_ref.dtype)
            else:
                o_ref[:, c0:c0 + tn] = y.astype(o_ref.dtype)
        col += width


def _project(x2d, w_bf16, bias, rope_tabs, splits, rope_cols, seq, tn):
    n, d = x2d.shape
    nout = w_bf16.shape[1]
    tm = min(TM_PROJ, seq)
    assert n % tm == 0 and seq % tm == 0 and sum(splits) == nout
    assert all(s % tn == 0 for s in splits) and rope_cols % tn == 0 and tn % LANES == 0
    pos_blocks = seq // tm
    c_tab, sa_tab, sb_tab = rope_tabs
    tab_spec = pl.BlockSpec((tm, LANES), lambda i: (i % pos_blocks, 0))
    kern = functools.partial(_proj_kernel, splits=tuple(splits), rope_cols=rope_cols, tn=tn)
    return pl.pallas_call(
        kern,
        grid=(n // tm,),
        in_specs=[
            pl.BlockSpec((tm, d), lambda i: (i, 0)),
            pl.BlockSpec((d, nout), lambda i: (0, 0)),
            pl.BlockSpec((1, nout), lambda i: (0, 0)),
            tab_spec, tab_spec, tab_spec,
        ],
        out_specs=[pl.BlockSpec((tm, s), lambda i: (i, 0)) for s in splits],
        out_shape=[jax.ShapeDtypeStruct((n, s), BF16) for s in splits],
        compiler_params=_cparams(("parallel",)),
        name="proj_rope",
    )(x2d, w_bf16, bias.reshape(1, nout).astype(F32), c_tab, sa_tab, sb_tab)


def _rope_tables(seq):
    half = ROPE_DIM // 2
    inv = ROPE_THETA ** (-jnp.arange(0, ROPE_DIM, 2, dtype=F32) / ROPE_DIM)
    ang = jnp.arange(seq, dtype=F32)[:, None] * inv[None, :]
    cos, sin = jnp.cos(ang), jnp.sin(ang)
    ones = jnp.ones((seq, HEAD_DIM - ROPE_DIM), F32)
    zeros = jnp.zeros((seq, HEAD_DIM - ROPE_DIM), F32)
    zh = jnp.zeros((seq, half), F32)
    c_head = jnp.concatenate([cos, cos, ones], axis=1)
    sa_head = jnp.concatenate([-sin, zh, zeros], axis=1)
    sb_head = jnp.concatenate([zh, sin, zeros], axis=1)
    rep = LANES // HEAD_DIM
    return tuple(jnp.tile(t, (1, rep)) for t in (c_head, sa_head, sb_head))


def _kmean_kernel(k_ref, o_ref, *, nb, blk):
    k = k_ref[...].astype(F32)
    km = jnp.mean(k.reshape(nb, blk, HEAD_DIM), axis=1)
    o_ref[...] = jnp.zeros(o_ref.shape, F32)
    o_ref[0:nb, :] = km


def _kmean(k_hm, nb):
    b, hkv, s, _ = k_hm.shape
    assert nb <= LANES
    return pl.pallas_call(
        functools.partial(_kmean_kernel, nb=nb, blk=MOBA_BLOCK),
        grid=(b, hkv),
        in_specs=[pl.BlockSpec((None, None, s, HEAD_DIM), lambda i, j: (i, j, 0, 0))],
        out_specs=pl.BlockSpec((None, None, LANES, HEAD_DIM), lambda i, j: (i, j, 0, 0)),
        out_shape=jax.ShapeDtypeStruct((b, hkv, LANES, HEAD_DIM), F32),
        compiler_params=_cparams(("parallel", "parallel")),
        name="moba_kmean",
    )(k_hm)


def _moba_kernel(it_ref, jt_ref, q_ref, kt_ref, v_ref, km_ref, o_ref, m_scr, l_scr, acc_scr, sel_scr,
                 *, groups, blk, scale):
    p = pl.program_id(2)
    i = it_ref[p]
    j = jt_ref[p]
    is_diag = j == i
    lane = lax.broadcasted_iota(jnp.int32, (blk, LANES), 1)

    @pl.when(is_diag)
    def _():
        row = lax.broadcasted_iota(jnp.int32, (blk, blk), 0)
        col = lax.broadcasted_iota(jnp.int32, (blk, blk), 1)
        km = km_ref[...]
        for g in range(groups):
            qg = q_ref[g]
            gate = lax.dot_general(qg.astype(F32), km, (((1,), (1,)), ((), ())),
                                   precision=lax.Precision.HIGHEST, preferred_element_type=F32)
            cur = jnp.where(lane < i, gate, NEG_INF)
            sel = jnp.zeros((blk, LANES), F32)
            for _ in range(MOBA_TOPK):
                mx = jnp.max(cur, axis=-1, keepdims=True)
                idx = jnp.min(jnp.where(cur == mx, lane, LANES), axis=-1, keepdims=True)
                hit = lane == idx
                sel = jnp.where(hit & (mx > NEG_INF), 1.0, sel)
                cur = jnp.where(hit, NEG_INF, cur)
            sel_scr[g] = sel
            s = jnp.dot(qg, kt_ref[...], preferred_element_type=F32) * scale
            s = jnp.where(col <= row, s, NEG_INF)
            mx = jnp.max(s, axis=-1, keepdims=True)
            pr = jnp.exp(s - mx)
            m_scr[g] = mx
            l_scr[g] = jnp.sum(pr, axis=-1, keepdims=True)
            acc_scr[g] = jnp.dot(pr.astype(BF16), v_ref[...], preferred_element_type=F32)

    @pl.when(jnp.logical_not(is_diag))
    def _():
        for g in range(groups):
            qg = q_ref[g]
            rowsel = jnp.max(jnp.where(lane == j, sel_scr[g], 0.0), axis=-1, keepdims=True) > 0.0
            s = jnp.dot(qg, kt_ref[...], preferred_element_type=F32) * scale
            s = jnp.where(rowsel, s, NEG_INF)
            m_old = m_scr[g]
            m_new = jnp.maximum(m_old, jnp.max(s, axis=-1, keepdims=True))
            alpha = jnp.exp(m_old - m_new)
            pr = jnp.exp(s - m_new)
            l_scr[g] = alpha * l_scr[g] + jnp.sum(pr, axis=-1, keepdims=True)
            acc_scr[g] = alpha * acc_scr[g] + jnp.dot(pr.astype(BF16), v_ref[...], preferred_element_type=F32)
            m_scr[g] = m_new

    @pl.when((j == i - 1) | (i == 0))
    def _():
        for g in range(groups):
            o_ref[g] = (acc_scr[g] / l_scr[g]).astype(o_ref.dtype)


def _moba_attention(q_hm, kt_hm, v_hm, kmean):
    b, hkv, groups, s, _ = q_hm.shape
    blk = MOBA_BLOCK
    assert s % blk == 0
    nb = s // blk
    it, jt = [], []
    for i in range(nb):
        it.append(i)
        jt.append(i)
        for j in range(i):
            it.append(i)
            jt.append(j)
    it = jnp.asarray(np.asarray(it, np.int32))
    jt = jnp.asarray(np.asarray(jt, np.int32))
    kern = functools.partial(_moba_kernel, groups=groups, blk=blk, scale=HEAD_DIM ** -0.5)
    grid_spec = pltpu.PrefetchScalarGridSpec(
        num_scalar_prefetch=2,
        grid=(b, hkv, int(it.shape[0])),
        in_specs=[
            pl.BlockSpec((None, None, groups, blk, HEAD_DIM), lambda bi, h, p, it, jt: (bi, h, 0, it[p], 0)),
            pl.BlockSpec((None, None, HEAD_DIM, blk), lambda bi, h, p, it, jt: (bi, h, 0, jt[p])),
            pl.BlockSpec((None, None, blk, HEAD_DIM), lambda bi, h, p, it, jt: (bi, h, jt[p], 0)),
            pl.BlockSpec((None, None, LANES, HEAD_DIM), lambda bi, h, p, it, jt: (bi, h, 0, 0)),
        ],
        out_specs=pl.BlockSpec((None, None, groups, blk, HEAD_DIM), lambda bi, h, p, it, jt: (bi, h, 0, it[p], 0)),
        scratch_shapes=[
            pltpu.VMEM((groups, blk, 1), F32),
            pltpu.VMEM((groups, blk, 1), F32),
            pltpu.VMEM((groups, blk, HEAD_DIM), F32),
            pltpu.VMEM((groups, blk, LANES), F32),
        ],
    )
    return pl.pallas_call(
        kern,
        grid_spec=grid_spec,
        out_shape=jax.ShapeDtypeStruct(q_hm.shape, BF16),
        compiler_params=_cparams(("parallel", "parallel", "arbitrary")),
        name="moba_attn",
    )(it, jt, q_hm, kt_hm, v_hm, kmean)


def _swa_kernel(q_ref, ktp_ref, ktc_ref, vp_ref, vc_ref, sink_ref, o_ref, *, groups, win, scale):
    kh = pl.program_id(1)
    i = pl.program_id(2)
    row = lax.broadcasted_iota(jnp.int32, (win, win), 0)
    col = lax.broadcasted_iota(jnp.int32, (win, win), 1)
    cur_mask = col <= row
    prev_mask = (col > row) & (i > 0)
    for g in range(groups):
        qg = q_ref[g]
        sp = jnp.dot(qg, ktp_ref[...], preferred_element_type=F32) * scale
        sc = jnp.dot(qg, ktc_ref[...], preferred_element_type=F32) * scale
        sp = jnp.where(prev_mask, sp, NEG_INF)
        sc = jnp.where(cur_mask, sc, NEG_INF)
        sink = sink_ref[kh * groups + g]
        mx = jnp.maximum(jnp.maximum(jnp.max(sp, axis=-1, keepdims=True), jnp.max(sc, axis=-1, keepdims=True)), sink)
        pp = jnp.exp(sp - mx)
        pc = jnp.exp(sc - mx)
        den = jnp.sum(pp, axis=-1, keepdims=True) + jnp.sum(pc, axis=-1, keepdims=True) + jnp.exp(sink - mx)
        o = (jnp.dot(pp.astype(BF16), vp_ref[...], preferred_element_type=F32)
             + jnp.dot(pc.astype(BF16), vc_ref[...], preferred_element_type=F32))
        o_ref[g] = (o / den).astype(o_ref.dtype)


def _swa_attention(q_hm, kt_hm, v_hm, sinks):
    b, hkv, groups, s, _ = q_hm.shape
    win = SWA_WINDOW
    assert s % win == 0
    nb = s // win
    kern = functools.partial(_swa_kernel, groups=groups, win=win, scale=HEAD_DIM ** -0.5)
    return pl.pallas_call(
        kern,
        grid=(b, hkv, nb),
        in_specs=[
            pl.BlockSpec((None, None, groups, win, HEAD_DIM), lambda bi, h, i: (bi, h, 0, i, 0)),
            pl.BlockSpec((None, None, HEAD_DIM, win), lambda bi, h, i: (bi, h, 0, jnp.maximum(i - 1, 0))),
            pl.BlockSpec((None, None, HEAD_DIM, win), lambda bi, h, i: (bi, h, 0, i)),
            pl.BlockSpec((None, None, win, HEAD_DIM), lambda bi, h, i: (bi, h, jnp.maximum(i - 1, 0), 0)),
            pl.BlockSpec((None, None, win, HEAD_DIM), lambda bi, h, i: (bi, h, i, 0)),
            pl.BlockSpec(memory_space=pltpu.SMEM),
        ],
        out_specs=pl.BlockSpec((None, None, groups, win, HEAD_DIM), lambda bi, h, i: (bi, h, 0, i, 0)),
        out_shape=jax.ShapeDtypeStruct(q_hm.shape, BF16),
        compiler_params=_cparams(("parallel", "parallel", "parallel")),
        name="swa_attn",
    )(q_hm, kt_hm, kt_hm, v_hm, v_hm, sinks.astype(F32))


def _layer_norm_rows(z, g, b):
    mu = jnp.mean(z, axis=-1, keepdims=True)
    zc = z - mu
    var = jnp.mean(zc * zc, axis=-1, keepdims=True)
    return zc * lax.rsqrt(var + LN_EPS) * g + b


def _oproj_kernel(a_ref, x_ref, wo_ref, bo_ref, g_ref, b_ref, wr_ref, br_ref, x1_ref, ti_ref, tg_ref,
                  *, alpha, n_experts):
    mix = jnp.dot(a_ref[...], wo_ref[...], preferred_element_type=F32) + bo_ref[...]
    x1 = _layer_norm_rows(alpha * x_ref[...] + mix, g_ref[...], b_ref[...])
    x1_ref[...] = x1
    logits = jnp.dot(x1, wr_ref[...], precision=lax.Precision.HIGHEST, preferred_element_type=F32) + br_ref[...]
    lane = lax.broadcasted_iota(jnp.int32, logits.shape, 1)
    cur = jnp.where(lane < n_experts, logits, NEG_INF)
    vals, idxs = [], []
    for _ in range(TOP_K):
        mx = jnp.max(cur, axis=-1, keepdims=True)
        idx = jnp.min(jnp.where(cur == mx, lane, LANES), axis=-1, keepdims=True)
        vals.append(mx)
        idxs.append(idx)
        cur = jnp.where(lane == idx, NEG_INF, cur)
    exps = [jnp.exp(v - vals[0]) for v in vals]
    den = exps[0]
    for e in exps[1:]:
        den = den + e
    ti = jnp.zeros(logits.shape, jnp.int32)
    tg = jnp.zeros(logits.shape, F32)
    for k in range(TOP_K):
        ti = jnp.where(lane == k, idxs[k], ti)
        tg = jnp.where(lane == k, exps[k] / den, tg)
    ti_ref[...] = ti
    tg_ref[...] = tg


def _oproj_ln_router(attn2d, x2d, wo_bf16, bo, ln_g, ln_b, w_r, b_r, alpha):
    n, d = x2d.shape
    da = attn2d.shape[1]
    tm = min(TM_OPROJ, n)
    assert n % tm == 0
    n_experts = w_r.shape[1]
    wr_pad = jnp.zeros((d, LANES), F32).at[:, :n_experts].set(w_r.astype(F32))
    br_pad = jnp.zeros((1, LANES), F32).at[0, :n_experts].set(b_r.astype(F32))
    row = lambda i: (i, 0)
    fixed = lambda i: (0, 0)
    return pl.pallas_call(
        functools.partial(_oproj_kernel, alpha=alpha, n_experts=n_experts),
        grid=(n // tm,),
        in_specs=[
            pl.BlockSpec((tm, da), row),
            pl.BlockSpec((tm, d), row),
            pl.BlockSpec((da, d), fixed),
            pl.BlockSpec((1, d), fixed),
            pl.BlockSpec((1, d), fixed),
            pl.BlockSpec((1, d), fixed),
            pl.BlockSpec((d, LANES), fixed),
            pl.BlockSpec((1, LANES), fixed),
        ],
        out_specs=[pl.BlockSpec((tm, d), row), pl.BlockSpec((tm, LANES), row), pl.BlockSpec((tm, LANES), row)],
        out_shape=[jax.ShapeDtypeStruct((n, d), F32), jax.ShapeDtypeStruct((n, LANES), jnp.int32),
                   jax.ShapeDtypeStruct((n, LANES), F32)],
        compiler_params=_cparams(("parallel",)),
        name="oproj_ln_router",
    )(attn2d, x2d, wo_bf16, bo.reshape(1, d).astype(F32), ln_g.reshape(1, d).astype(F32),
      ln_b.reshape(1, d).astype(F32), wr_pad, br_pad)


def _moe_gemm_kernel(be_ref, nu_ref, rt_hbm, x_hbm, wg_ref, wu_ref, bg_ref, bu_ref, wd_ref, bd_ref, o_ref,
                     idx_smem, xg, isem, gsem, *, tm, hidden, hc):
    del be_ref
    i = pl.program_id(0)
    slot = i % 2
    n_used = nu_ref[0]

    def idx_copy(blk, s):
        return pltpu.make_async_copy(rt_hbm.at[blk], idx_smem.at[s], isem.at[s])

    def issue_gather(s):
        def body(r, carry):
            tok = idx_smem[s, r]
            pltpu.make_async_copy(x_hbm.at[pl.ds(tok, 1)], xg.at[s, pl.ds(r, 1)], gsem.at[s]).start()
            return carry
        lax.fori_loop(0, tm, body, 0)

    def wait_gather(s):
        pltpu.make_async_copy(x_hbm.at[pl.ds(0, tm)], xg.at[s], gsem.at[s]).wait()

    @pl.when(i == 0)
    def _():
        idx_copy(0, 0).start()
        idx_copy(0, 0).wait()
        issue_gather(0)

        @pl.when(1 < n_used)
        def _():
            idx_copy(1, 1).start()

    @pl.when(i + 1 < n_used)
    def _():
        idx_copy(i + 1, 1 - slot).wait()
        issue_gather(1 - slot)

    @pl.when(i + 2 < n_used)
    def _():
        idx_copy(i + 2, slot).start()

    @pl.when(i < n_used)
    def _():
        wait_gather(slot)
        xb = xg[slot].astype(BF16)
        for c in range(0, hidden, hc):
            gl = jnp.dot(xb, wg_ref[:, c:c + hc], preferred_element_type=F32) + bg_ref[:, c:c + hc]
            ln = jnp.dot(xb, wu_ref[:, c:c + hc], preferred_element_type=F32) + bu_ref[:, c:c + hc]
            gl = jnp.minimum(gl, SWIGLU_LIMIT)
            ln = jnp.clip(ln, -SWIGLU_LIMIT, SWIGLU_LIMIT)
            act = gl * jax.nn.sigmoid(SWIGLU_ALPHA * gl) * (ln + 1.0)
            part = jnp.dot(act.astype(BF16), wd_ref[c:c + hc, :], preferred_element_type=F32)
            if c == 0:
                o_ref[...] = part + bd_ref[...]
            else:
                o_ref[...] += part

    @pl.when(i >= n_used)
    def _():
        o_ref[...] = jnp.zeros(o_ref.shape, o_ref.dtype)


def _moe_gemm(x2d, row_tok, block_expert, n_used, wg, wu, bg, bu, wd, bd, tm):
    n, d = x2d.shape
    n_blocks = row_tok.shape[0]
    hidden = wg.shape[2]
    hc = min(HC_MOE, hidden)
    assert hidden % hc == 0
    w_in = lambda i, be, nu: (be[i], 0, 0)
    grid_spec = pltpu.PrefetchScalarGridSpec(
        num_scalar_prefetch=2,
        grid=(n_blocks,),
        in_specs=[
            pl.BlockSpec(memory_space=pl.ANY),
            pl.BlockSpec(memory_space=pl.ANY),
            pl.BlockSpec((None, d, hidden), w_in),
            pl.BlockSpec((None, d, hidden), w_in),
            pl.BlockSpec((None, 1, hidden), w_in),
            pl.BlockSpec((None, 1, hidden), w_in),
            pl.BlockSpec((None, hidden, d), w_in),
            pl.BlockSpec((None, 1, d), w_in),
        ],
        out_specs=pl.BlockSpec((tm, d), lambda i, be, nu: (i, 0)),
        scratch_shapes=[
            pltpu.SMEM((2, tm), jnp.int32),
            pltpu.VMEM((2, tm, d), F32),
            pltpu.SemaphoreType.DMA((2,)),
            pltpu.SemaphoreType.DMA((2,)),
        ],
    )
    return pl.pallas_call(
        functools.partial(_moe_gemm_kernel, tm=tm, hidden=hidden, hc=hc),
        grid_spec=grid_spec,
        out_shape=jax.ShapeDtypeStruct((n_blocks * tm, d), F32),
        compiler_params=_cparams(("arbitrary",)),
        name="moe_gemm",
    )(block_expert, n_used, row_tok, x2d, wg, wu, bg, bu, wd, bd)


def _dispatch_tables(top_idx, tm, n_blocks):
    n = top_idx.shape[0]
    e_flat = top_idx.reshape(-1)
    onehot = (e_flat[:, None] == jnp.arange(N_EXPERTS, dtype=jnp.int32)[None, :]).astype(jnp.int32)
    csum = jnp.cumsum(onehot, axis=0)
    counts = csum[-1]
    rank = jnp.sum((csum - onehot) * onehot, axis=1)
    blocks_e = (counts + tm - 1) // tm
    bend = jnp.cumsum(blocks_e)
    bstart = bend - blocks_e
    dest = jnp.sum(onehot * bstart[None, :], axis=1) * tm + rank
    tok_flat = jnp.repeat(jnp.arange(n, dtype=jnp.int32), TOP_K)
    row_tok = jnp.zeros((n_blocks * tm,), jnp.int32).at[dest].set(tok_flat).reshape(n_blocks, tm)
    block_expert = jnp.minimum(
        jnp.searchsorted(bend, jnp.arange(n_blocks, dtype=jnp.int32), side="right"), N_EXPERTS - 1).astype(jnp.int32)
    n_used = bend[-1:].astype(jnp.int32)
    return dest.reshape(n, TOP_K).astype(jnp.int32), row_tok, block_expert, n_used


def _combine_kernel(dest_hbm, yr_hbm, x_ref, tg_ref, g_ref, b_ref, o_ref, idx_smem, yg, isem, gsem,
                    *, tm, n_steps, alpha):
    i = pl.program_id(0)
    slot = i % 2

    def idx_copy(step, s):
        return pltpu.make_async_copy(dest_hbm.at[step], idx_smem.at[s], isem.at[s])

    def issue_gather(s):
        def body(r, carry):
            for k in range(TOP_K):
                src = idx_smem[s, k * tm + r]
                pltpu.make_async_copy(yr_hbm.at[pl.ds(src, 1)], yg.at[s, k, pl.ds(r, 1)], gsem.at[s]).start()
            return carry
        lax.fori_loop(0, tm, body, 0)

    def wait_gather(s):
        for k in range(TOP_K):
            pltpu.make_async_copy(yr_hbm.at[pl.ds(0, tm)], yg.at[s, k], gsem.at[s]).wait()

    @pl.when(i == 0)
    def _():
        idx_copy(0, 0).start()
        idx_copy(0, 0).wait()
        issue_gather(0)
        if n_steps > 1:
            idx_copy(1, 1).start()

    @pl.when(i + 1 < n_steps)
    def _():
        idx_copy(i + 1, 1 - slot).wait()
        issue_gather(1 - slot)

    @pl.when(i + 2 < n_steps)
    def _():
        idx_copy(i + 2, slot).start()

    wait_gather(slot)
    tg = tg_ref[...]
    y = tg[:, 0:1] * yg[slot, 0]
    for k in range(1, TOP_K):
        y = y + tg[:, k:k + 1] * yg[slot, k]
    o_ref[...] = _layer_norm_rows(alpha * x_ref[...] + y, g_ref[...], b_ref[...])


def _combine_ln(dest, yr, x1, gates, ln_g, ln_b, alpha):
    n, d = x1.shape
    tm = min(TM_COMBINE, n)
    assert n % tm == 0
    n_steps = n // tm
    dest_steps = dest.reshape(n_steps, tm, TOP_K).transpose(0, 2, 1).reshape(n_steps, TOP_K * tm)
    row = lambda i: (i, 0)
    fixed = lambda i: (0, 0)
    return pl.pallas_call(
        functools.partial(_combine_kernel, tm=tm, n_steps=n_steps, alpha=alpha),
        grid=(n_steps,),
        in_specs=[
            pl.BlockSpec(memory_space=pl.ANY),
            pl.BlockSpec(memory_space=pl.ANY),
            pl.BlockSpec((tm, d), row),
            pl.BlockSpec((tm, LANES), row),
            pl.BlockSpec((1, d), fixed),
            pl.BlockSpec((1, d), fixed),
        ],
        out_specs=pl.BlockSpec((tm, d), row),
        out_shape=jax.ShapeDtypeStruct((n, d), F32),
        scratch_shapes=[
            pltpu.SMEM((2, TOP_K * tm), jnp.int32),
            pltpu.VMEM((2, TOP_K, tm, d), F32),
            pltpu.SemaphoreType.DMA((2,)),
            pltpu.SemaphoreType.DMA((2,)),
        ],
        compiler_params=_cparams(("arbitrary",)),
        name="moe_combine_ln",
    )(dest_steps, yr, x1, gates, ln_g.reshape(1, d).astype(F32), ln_b.reshape(1, d).astype(F32))


def _moe_layer(x1, top_idx, gates, w_gate_up, b_gate_up, w_down, b_down, ln_g, ln_b, alpha):
    n, d = x1.shape
    tm = min(TM_MOE, n)
    n_blocks = -(-(n * TOP_K) // tm) + N_EXPERTS
    dest, row_tok, block_expert, n_used = _dispatch_tables(top_idx[:, :TOP_K], tm, n_blocks)
    hidden = w_down.shape[1]
    wg = w_gate_up[:, :, 0::2].astype(BF16)
    wu = w_gate_up[:, :, 1::2].astype(BF16)
    bg = b_gate_up[:, 0::2].reshape(N_EXPERTS, 1, hidden).astype(F32)
    bu = b_gate_up[:, 1::2].reshape(N_EXPERTS, 1, hidden).astype(F32)
    wd = w_down.astype(BF16)
    bd = b_down.reshape(N_EXPERTS, 1, d).astype(F32)
    yr = _moe_gemm(x1, row_tok, block_expert, n_used, wg, wu, bg, bu, wd, bd, tm)
    return _combine_ln(dest, yr, x1, gates, ln_g, ln_b, alpha)


def _to_heads(t2d, batch, seq, hkv, groups):
    return t2d.reshape(batch, seq, hkv, groups, HEAD_DIM).transpose(0, 2, 3, 1, 4)


def _from_heads(t_hm):
    b, hkv, groups, s, _ = t_hm.shape
    return t_hm.transpose(0, 3, 1, 2, 4).reshape(b * s, hkv * groups * HEAD_DIM)


def kernel(x, w_qkv_a, w_o_a, w_q_b, b_q_b, sinks_b, w_o_b, b_o_b, w_kv_shared, b_kv_shared, ln_mix_g, ln_mix_b, ln_ffn_g, ln_ffn_b, w_router, b_router, w_gate_up, b_gate_up, w_down, b_down):
    batch, seq, d = x.shape
    depth = ln_mix_g.shape[0]
    n_a = w_qkv_a.shape[0]
    alpha = (2 * depth) ** 0.25
    attn_w = N_HEADS * HEAD_DIM
    kv_a_w = N_KV_A * HEAD_DIM
    kv_b_w = N_KV_B * HEAD_DIM
    ga, gb = N_HEADS // N_KV_A, N_HEADS // N_KV_B
    rope = _rope_tables(seq)
    xs = x.reshape(batch * seq, d).astype(F32)
    kt_sh = v_sh = None
    for layer in range(depth):
        if layer < n_a:
            q, k, v = _project(xs, w_qkv_a[layer].astype(BF16), jnp.zeros((attn_w + 2 * kv_a_w,), F32), rope,
                               (attn_w, kv_a_w, kv_a_w), attn_w + kv_a_w, seq, tn=512)
            k_hm = _to_heads(k, batch, seq, N_KV_A, 1)[:, :, 0]
            v_hm = _to_heads(v, batch, seq, N_KV_A, 1)[:, :, 0]
            kmean = _kmean(k_hm, seq // MOBA_BLOCK)
            o_hm = _moba_attention(_to_heads(q, batch, seq, N_KV_A, ga), k_hm.transpose(0, 1, 3, 2), v_hm, kmean)
            wo, bo = w_o_a[layer], jnp.zeros((d,), F32)
        else:
            jb = layer - n_a
            if jb == 0:
                k, v = _project(xs, w_kv_shared.astype(BF16), b_kv_shared, rope, (kv_b_w, kv_b_w), kv_b_w, seq,
                                tn=256)
                kt_sh = _to_heads(k, batch, seq, N_KV_B, 1)[:, :, 0].transpose(0, 1, 3, 2)
                v_sh = _to_heads(v, batch, seq, N_KV_B, 1)[:, :, 0]
            (q,) = _project(xs, w_q_b[jb].astype(BF16), b_q_b[jb], rope, (attn_w,), attn_w, seq, tn=512)
            o_hm = _swa_attention(_to_heads(q, batch, seq, N_KV_B, gb), kt_sh, v_sh, sinks_b[jb])
            wo, bo = w_o_b[jb], b_o_b[jb]
        x1, top_idx, gates = _oproj_ln_router(_from_heads(o_hm), xs, wo.astype(BF16), bo, ln_mix_g[layer],
                                              ln_mix_b[layer], w_router[layer], b_router[layer], alpha)
        xs = _moe_layer(x1, top_idx, gates, w_gate_up[layer], b_gate_up[layer], w_down[layer], b_down[layer],
                        ln_ffn_g[layer], ln_ffn_b[layer], alpha)
    return xs.reshape(batch, seq, d).astype(x.dtype)
```

```python
import functools

import numpy as np
import jax
import jax.numpy as jnp
from jax import lax
from jax.experimental import pallas as pl
from jax.experimental.pallas import tpu as pltpu

F32 = jnp.float32
BF16 = jnp.bfloat16
NEG_INF = float("-inf")

HEAD_DIM = 64
N_HEADS = 32
N_KV_A = 8
N_KV_B = 4
ROPE_DIM = 16
ROPE_THETA = 500000.0
MOBA_BLOCK = 256
MOBA_TOPK = 3
SWA_WINDOW = 128
N_EXPERTS = 32
TOP_K = 4
SWIGLU_LIMIT = 7.0
SWIGLU_ALPHA = 1.702
LN_EPS = 1e-5

MOBA_MASK_BIAS = -(2.0 ** 100)

LANES = 128
V7X_VMEM_BYTES = 64 * 1024 * 1024
VMEM_LIMIT = V7X_VMEM_BYTES - 8 * 1024 * 1024

TM_PROJ = 512
TM_OPROJ = 256
TM_MOE = 512
TM_COMBINE = 256
HC_MOE = 512


def _cparams(semantics):
    return pltpu.CompilerParams(dimension_semantics=semantics, vmem_limit_bytes=VMEM_LIMIT)


def _proj_kernel(x_ref, w_ref, b_ref, c_ref, sa_ref, sb_ref, *o_refs, splits, rope_cols, tn):
    x = x_ref[...].astype(BF16)
    col = 0
    for o_ref, width in zip(o_refs, splits):
        for c0 in range(0, width, tn):
            lo = col + c0
            y = jnp.dot(x, w_ref[:, lo:lo + tn], preferred_element_type=F32) + b_ref[:, lo:lo + tn]
            if lo < rope_cols:
                for l0 in range(0, tn, LANES):
                    seg = y[:, l0:l0 + LANES]
                    rot = (seg * c_ref[...] + pltpu.roll(seg, LANES - ROPE_DIM // 2, 1) * sa_ref[...]
                           + pltpu.roll(seg, ROPE_DIM // 2, 1) * sb_ref[...])
                    o_ref[:, c0 + l0:c0 + l0 + LANES] = rot.astype(o_ref.dtype)
            else:
                o_ref[:, c0:c0 + tn] = y.astype(o_ref.dtype)
        col += width


def _project(x2d, w_bf16, bias, rope_tabs, splits, rope_cols, seq, tn):
    n, d = x2d.shape
    nout = w_bf16.shape[1]
    tm = min(TM_PROJ, seq)
    assert n % tm == 0 and seq % tm == 0 and sum(splits) == nout
    assert all(s % tn == 0 for s in splits) and rope_cols % tn == 0 and tn % LANES == 0
    pos_blocks = seq // tm
    c_tab, sa_tab, sb_tab = rope_tabs
    tab_spec = pl.BlockSpec((tm, LANES), lambda i: (i % pos_blocks, 0))
    kern = functools.partial(_proj_kernel, splits=tuple(splits), rope_cols=rope_cols, tn=tn)
    return pl.pallas_call(
        kern,
        grid=(n // tm,),
        in_specs=[
            pl.BlockSpec((tm, d), lambda i: (i, 0)),
            pl.BlockSpec((d, nout), lambda i: (0, 0)),
            pl.BlockSpec((1, nout), lambda i: (0, 0)),
            tab_spec, tab_spec, tab_spec,
        ],
        out_specs=[pl.BlockSpec((tm, s), lambda i: (i, 0)) for s in splits],
        out_shape=[jax.ShapeDtypeStruct((n, s), BF16) for s in splits],
        compiler_params=_cparams(("parallel",)),
        name="proj_rope",
    )(x2d, w_bf16, bias.reshape(1, nout).astype(F32), c_tab, sa_tab, sb_tab)


def _rope_tables(seq):
    half = ROPE_DIM // 2
    inv = ROPE_THETA ** (-jnp.arange(0, ROPE_DIM, 2, dtype=F32) / ROPE_DIM)
    ang = jnp.arange(seq, dtype=F32)[:, None] * inv[None, :]
    cos, sin = jnp.cos(ang), jnp.sin(ang)
    ones = jnp.ones((seq, HEAD_DIM - ROPE_DIM), F32)
    zeros = jnp.zeros((seq, HEAD_DIM - ROPE_DIM), F32)
    zh = jnp.zeros((seq, half), F32)
    c_head = jnp.concatenate([cos, cos, ones], axis=1)
    sa_head = jnp.concatenate([-sin, zh, zeros], axis=1)
    sb_head = jnp.concatenate([zh, sin, zeros], axis=1)
    rep = LANES // HEAD_DIM
    return tuple(jnp.tile(t, (1, rep)) for t in (c_head, sa_head, sb_head))


def _to_heads(t2d, batch, seq, hkv, groups):
    return t2d.reshape(batch, seq, hkv, groups, HEAD_DIM).transpose(0, 2, 3, 1, 4)


def _from_heads(t_hm):
    b, hkv, groups, s, _ = t_hm.shape
    return t_hm.transpose(0, 3, 1, 2, 4).reshape(b * s, hkv * groups * HEAD_DIM)


def _kmean_kernel(k_ref, o_ref, *, nb, blk):
    k = k_ref[...].astype(F32)
    km = jnp.mean(k.reshape(nb, blk, HEAD_DIM), axis=1)
    o_ref[...] = jnp.zeros(o_ref.shape, F32)
    o_ref[0:nb, 0:HEAD_DIM] = km


def _kmean(k_hm, nb):
    b, hkv, s, _ = k_hm.shape
    assert nb <= LANES
    return pl.pallas_call(
        functools.partial(_kmean_kernel, nb=nb, blk=MOBA_BLOCK),
        grid=(b, hkv),
        in_specs=[pl.BlockSpec((None, None, s, HEAD_DIM), lambda i, j: (i, j, 0, 0))],
        out_specs=pl.BlockSpec((None, None, LANES, LANES), lambda i, j: (i, j, 0, 0)),
        out_shape=jax.ShapeDtypeStruct((b, hkv, LANES, LANES), F32),
        compiler_params=_cparams(("parallel", "parallel")),
        name="moba_kmean",
    )(k_hm)


def _moba_kernel(it_ref, jt_ref, q_ref, kta_ref, va_ref, km_ref, o_ref, qa_scr, m_scr, acc_scr,
                 *, groups, blk, scale):
    p = pl.program_id(2)
    i = it_ref[p]
    j = jt_ref[p]
    is_diag = j == i

    def softmax_step(g, s, m_old, first):
        s0, s1 = s[:, :LANES], s[:, LANES:]
        rm = jnp.max(jnp.maximum(s0, s1), axis=-1, keepdims=True)
        if first:
            m_new = jnp.broadcast_to(rm, (blk, LANES))
        else:
            m_new = jnp.maximum(m_old, rm)
        pr = jnp.concatenate([jnp.exp(s0 - m_new), jnp.exp(s1 - m_new)], axis=1).astype(BF16)
        pv = jnp.dot(pr, va_ref[...], preferred_element_type=F32)
        if first:
            acc_scr[g] = pv
        else:
            acc_scr[g] = jnp.exp(m_old - m_new) * acc_scr[g] + pv
        m_scr[g] = m_new

    @pl.when(is_diag)
    def _():
        row = lax.broadcasted_iota(jnp.int32, (blk, blk), 0)
        col = lax.broadcasted_iota(jnp.int32, (blk, blk), 1)
        lane = lax.broadcasted_iota(jnp.int32, (blk, LANES), 1)
        lane_f = lane.astype(F32)
        km = km_ref[...]
        for g in range(groups):
            qf = q_ref[g].astype(F32)
            gate = lax.dot_general(qf, km, (((1,), (1,)), ((), ())),
                                   precision=lax.Precision.HIGHEST, preferred_element_type=F32)
            cur = jnp.where(lane < i, gate, NEG_INF)
            sel = jnp.zeros((blk, LANES), F32)
            for _ in range(MOBA_TOPK):
                mx = jnp.max(cur, axis=-1, keepdims=True)
                idx = jnp.min(jnp.where(cur == mx, lane_f, float(LANES)), axis=-1, keepdims=True)
                hit = lane_f == idx
                sel = jnp.where(hit & (mx > NEG_INF), 1.0, sel)
                cur = jnp.where(hit, NEG_INF, cur)
            bias = pltpu.roll(jnp.where(sel > 0.0, 0.0, MOBA_MASK_BIAS), HEAD_DIM, 1)
            bias = jnp.where(lane >= HEAD_DIM, bias, 0.0)
            qs = qf * scale
            qa_scr[g] = (qs + bias).astype(BF16)
            s = jnp.dot(qs.astype(BF16), kta_ref[...], preferred_element_type=F32)
            s = jnp.where(col <= row, s, NEG_INF)
            softmax_step(g, s, None, True)

    @pl.when(jnp.logical_not(is_diag))
    def _():
        for g in range(groups):
            s = jnp.dot(qa_scr[g], kta_ref[...], preferred_element_type=F32)
            softmax_step(g, s, m_scr[g], False)

    @pl.when((j == i - 1) | (i == 0))
    def _():
        for g in range(groups):
            acc = acc_scr[g]
            o_ref[g] = (acc[:, :HEAD_DIM] / acc[:, HEAD_DIM:HEAD_DIM + 1]).astype(o_ref.dtype)


def _moba_attention(q_pad, kta, va, kmean):
    b, hkv, groups, s, _ = q_pad.shape
    blk = MOBA_BLOCK
    assert s % blk == 0 and blk == 2 * LANES
    nb = s // blk
    assert nb <= LANES - HEAD_DIM
    it, jt = [], []
    for i in range(nb):
        it.append(i)
        jt.append(i)
        for j in range(i):
            it.append(i)
            jt.append(j)
    it = jnp.asarray(np.asarray(it, np.int32))
    jt = jnp.asarray(np.asarray(jt, np.int32))
    kern = functools.partial(_moba_kernel, groups=groups, blk=blk, scale=HEAD_DIM ** -0.5)
    grid_spec = pltpu.PrefetchScalarGridSpec(
        num_scalar_prefetch=2,
        grid=(b, hkv, int(it.shape[0])),
        in_specs=[
            pl.BlockSpec((None, None, groups, blk, LANES), lambda bi, h, p, it, jt: (bi, h, 0, it[p], 0)),
            pl.BlockSpec((None, None, LANES, blk), lambda bi, h, p, it, jt: (bi, h, 0, jt[p])),
            pl.BlockSpec((None, None, blk, LANES), lambda bi, h, p, it, jt: (bi, h, jt[p], 0)),
            pl.BlockSpec((None, None, LANES, LANES), lambda bi, h, p, it, jt: (bi, h, 0, 0)),
        ],
        out_specs=pl.BlockSpec((None, None, groups, blk, HEAD_DIM), lambda bi, h, p, it, jt: (bi, h, 0, it[p], 0)),
        scratch_shapes=[
            pltpu.VMEM((groups, blk, LANES), BF16),
            pltpu.VMEM((groups, blk, LANES), F32),
            pltpu.VMEM((groups, blk, LANES), F32),
        ],
    )
    return pl.pallas_call(
        kern,
        grid_spec=grid_spec,
        out_shape=jax.ShapeDtypeStruct((b, hkv, groups, s, HEAD_DIM), BF16),
        compiler_params=_cparams(("parallel", "parallel", "arbitrary")),
        name="moba_attn",
    )(it, jt, q_pad, kta, va, kmean)


def _moba_operands(q2d, k2d, v2d, batch, seq, hkv, groups):
    q_hm = _to_heads(q2d, batch, seq, hkv, groups)
    q_pad = jnp.concatenate([q_hm, jnp.zeros_like(q_hm)], axis=-1)
    k_hm = _to_heads(k2d, batch, seq, hkv, 1)[:, :, 0]
    v_hm = _to_heads(v2d, batch, seq, hkv, 1)[:, :, 0]
    blk_of_key = jnp.arange(seq, dtype=jnp.int32) // MOBA_BLOCK
    onehot = (jnp.arange(LANES - HEAD_DIM, dtype=jnp.int32)[:, None] == blk_of_key[None, :]).astype(BF16)
    kta = jnp.concatenate([k_hm.transpose(0, 1, 3, 2),
                           jnp.broadcast_to(onehot, (batch, hkv) + onehot.shape)], axis=2)
    ones = jnp.ones(v_hm.shape[:-1] + (1,), BF16)
    zeros = jnp.zeros(v_hm.shape[:-1] + (LANES - HEAD_DIM - 1,), BF16)
    va = jnp.concatenate([v_hm, ones, zeros], axis=-1)
    return q_pad, k_hm, kta, va


def _swa_kernel(q_ref, ktp_ref, ktc_ref, vp_ref, vc_ref, sink_ref, o_ref, *, groups, win, scale):
    kh = pl.program_id(1)
    i = pl.program_id(2)
    row = lax.broadcasted_iota(jnp.int32, (win, win), 0)
    col = lax.broadcasted_iota(jnp.int32, (win, win), 1)
    cur_mask = col <= row
    prev_mask = (col > row) & (i > 0)
    for g in range(groups):
        qg = q_ref[g]
        sp = jnp.dot(qg, ktp_ref[...], preferred_element_type=F32) * scale
        sc = jnp.dot(qg, ktc_ref[...], preferred_element_type=F32) * scale
        sp = jnp.where(prev_mask, sp, NEG_INF)
        sc = jnp.where(cur_mask, sc, NEG_INF)
        sink = sink_ref[kh * groups + g]
        mx = jnp.maximum(jnp.maximum(jnp.max(sp, axis=-1, keepdims=True), jnp.max(sc, axis=-1, keepdims=True)), sink)
        pp = jnp.exp(sp - mx)
        pc = jnp.exp(sc - mx)
        den = jnp.sum(pp, axis=-1, keepdims=True) + jnp.sum(pc, axis=-1, keepdims=True) + jnp.exp(sink - mx)
        o = (jnp.dot(pp.astype(BF16), vp_ref[...], preferred_element_type=F32)
             + jnp.dot(pc.astype(BF16), vc_ref[...], preferred_element_type=F32))
        o_ref[g] = (o / den).astype(o_ref.dtype)


def _swa_attention(q_hm, kt_hm, v_hm, sinks):
    b, hkv, groups, s, _ = q_hm.shape
    win = SWA_WINDOW
    assert s % win == 0
    nb = s // win
    kern = functools.partial(_swa_kernel, groups=groups, win=win, scale=HEAD_DIM ** -0.5)
    return pl.pallas_call(
        kern,
        grid=(b, hkv, nb),
        in_specs=[
            pl.BlockSpec((None, None, groups, win, HEAD_DIM), lambda bi, h, i: (bi, h, 0, i, 0)),
            pl.BlockSpec((None, None, HEAD_DIM, win), lambda bi, h, i: (bi, h, 0, jnp.maximum(i - 1, 0))),
            pl.BlockSpec((None, None, HEAD_DIM, win), lambda bi, h, i: (bi, h, 0, i)),
            pl.BlockSpec((None, None, win, HEAD_DIM), lambda bi, h, i: (bi, h, jnp.maximum(i - 1, 0), 0)),
            pl.BlockSpec((None, None, win, HEAD_DIM), lambda bi, h, i: (bi, h, i, 0)),
            pl.BlockSpec(memory_space=pltpu.SMEM),
        ],
        out_specs=pl.BlockSpec((None, None, groups, win, HEAD_DIM), lambda bi, h, i: (bi, h, 0, i, 0)),
        out_shape=jax.ShapeDtypeStruct(q_hm.shape, BF16),
        compiler_params=_cparams(("parallel", "parallel", "parallel")),
        name="swa_attn",
    )(q_hm, kt_hm, kt_hm, v_hm, v_hm, sinks.astype(F32))


def _layer_norm_rows(z, g, b):
    mu = jnp.mean(z, axis=-1, keepdims=True)
    zc = z - mu
    var = jnp.mean(zc * zc, axis=-1, keepdims=True)
    return zc * lax.rsqrt(var + LN_EPS) * g + b


def _oproj_kernel(a_ref, x_ref, wo_ref, bo_ref, g_ref, b_ref, wr_ref, br_ref, x1_ref, ti_ref, tg_ref,
                  *, alpha, n_experts):
    mix = jnp.dot(a_ref[...], wo_ref[...], preferred_element_type=F32) + bo_ref[...]
    x1 = _layer_norm_rows(alpha * x_ref[...] + mix, g_ref[...], b_ref[...])
    x1_ref[...] = x1
    logits = jnp.dot(x1, wr_ref[...], precision=lax.Precision.HIGHEST, preferred_element_type=F32) + br_ref[...]
    lane = lax.broadcasted_iota(jnp.int32, logits.shape, 1)
    cur = jnp.where(lane < n_experts, logits, NEG_INF)
    vals, idxs = [], []
    for _ in range(TOP_K):
        mx = jnp.max(cur, axis=-1, keepdims=True)
        idx = jnp.min(jnp.where(cur == mx, lane, LANES), axis=-1, keepdims=True)
        vals.append(mx)
        idxs.append(idx)
        cur = jnp.where(lane == idx, NEG_INF, cur)
    exps = [jnp.exp(v - vals[0]) for v in vals]
    den = exps[0]
    for e in exps[1:]:
        den = den + e
    ti = jnp.zeros(logits.shape, jnp.int32)
    tg = jnp.zeros(logits.shape, F32)
    for k in range(TOP_K):
        ti = jnp.where(lane == k, idxs[k], ti)
        tg = jnp.where(lane == k, exps[k] / den, tg)
    ti_ref[...] = ti
    tg_ref[...] = tg


def _oproj_ln_router(attn2d, x2d, wo_bf16, bo, ln_g, ln_b, w_r, b_r, alpha):
    n, d = x2d.shape
    da = attn2d.shape[1]
    tm = min(TM_OPROJ, n)
    assert n % tm == 0
    n_experts = w_r.shape[1]
    wr_pad = jnp.zeros((d, LANES), F32).at[:, :n_experts].set(w_r.astype(F32))
    br_pad = jnp.zeros((1, LANES), F32).at[0, :n_experts].set(b_r.astype(F32))
    row = lambda i: (i, 0)
    fixed = lambda i: (0, 0)
    return pl.pallas_call(
        functools.partial(_oproj_kernel, alpha=alpha, n_experts=n_experts),
        grid=(n // tm,),
        in_specs=[
            pl.BlockSpec((tm, da), row),
            pl.BlockSpec((tm, d), row),
            pl.BlockSpec((da, d), fixed),
            pl.BlockSpec((1, d), fixed),
            pl.BlockSpec((1, d), fixed),
            pl.BlockSpec((1, d), fixed),
            pl.BlockSpec((d, LANES), fixed),
            pl.BlockSpec((1, LANES), fixed),
        ],
        out_specs=[pl.BlockSpec((tm, d), row), pl.BlockSpec((tm, LANES), row), pl.BlockSpec((tm, LANES), row)],
        out_shape=[jax.ShapeDtypeStruct((n, d), F32), jax.ShapeDtypeStruct((n, LANES), jnp.int32),
                   jax.ShapeDtypeStruct((n, LANES), F32)],
        compiler_params=_cparams(("parallel",)),
        name="oproj_ln_router",
    )(attn2d, x2d, wo_bf16, bo.reshape(1, d).astype(F32), ln_g.reshape(1, d).astype(F32),
      ln_b.reshape(1, d).astype(F32), wr_pad, br_pad)


def _moe_gemm_kernel(be_ref, nu_ref, rt_hbm, x_hbm, wgu_ref, bgu_ref, wd_ref, bd_ref, o_ref,
                     idx_smem, xg, isem, gsem, *, tm, hidden, hc):
    del be_ref
    i = pl.program_id(0)
    slot = i % 2
    n_used = nu_ref[0]

    def idx_copy(blk, s):
        return pltpu.make_async_copy(rt_hbm.at[blk], idx_smem.at[s], isem.at[s])

    def issue_gather(s):
        def body(r, carry):
            tok = idx_smem[s, r]
            pltpu.make_async_copy(x_hbm.at[pl.ds(tok, 1)], xg.at[s, pl.ds(r, 1)], gsem.at[s]).start()
            return carry
        lax.fori_loop(0, tm, body, 0, unroll=8)

    def wait_gather(s):
        pltpu.make_async_copy(x_hbm.at[pl.ds(0, tm)], xg.at[s], gsem.at[s]).wait()

    @pl.when(i == 0)
    def _():
        idx_copy(0, 0).start()
        idx_copy(0, 0).wait()
        issue_gather(0)

        @pl.when(1 < n_used)
        def _():
            idx_copy(1, 1).start()

    @pl.when(i + 1 < n_used)
    def _():
        idx_copy(i + 1, 1 - slot).wait()
        issue_gather(1 - slot)

    @pl.when(i + 2 < n_used)
    def _():
        idx_copy(i + 2, slot).start()

    @pl.when(i < n_used)
    def _():
        wait_gather(slot)
        xb = xg[slot].astype(BF16)
        even = lax.broadcasted_iota(jnp.int32, (tm, LANES), 1) % 2 == 0
        for c in range(0, hidden, hc):
            h = (jnp.dot(xb, wgu_ref[:, 2 * c:2 * (c + hc)], preferred_element_type=F32)
                 + bgu_ref[:, 2 * c:2 * (c + hc)])
            pieces = []
            for a in range(hc // LANES):
                acts = []
                for half in range(2):
                    hv = h[:, (2 * a + half) * LANES:(2 * a + half + 1) * LANES]
                    glu = jnp.minimum(hv, SWIGLU_LIMIT)
                    lin = jnp.clip(pltpu.roll(hv, LANES - 1, 1), -SWIGLU_LIMIT, SWIGLU_LIMIT)
                    acts.append(glu * jax.nn.sigmoid(SWIGLU_ALPHA * glu) * (lin + 1.0))
                pieces.append(jnp.where(even, acts[0], pltpu.roll(acts[1], 1, 1)).astype(BF16))
            act = jnp.concatenate(pieces, axis=1) if len(pieces) > 1 else pieces[0]
            part = jnp.dot(act, wd_ref[c:c + hc, :], preferred_element_type=F32)
            if c == 0:
                o_ref[...] = part + bd_ref[...]
            else:
                o_ref[...] += part

    @pl.when(i >= n_used)
    def _():
        o_ref[...] = jnp.zeros(o_ref.shape, o_ref.dtype)


def _moe_gemm(x2d, row_tok, block_expert, n_used, wgu, bgu, wd, bd, tm):
    n, d = x2d.shape
    n_blocks = row_tok.shape[0]
    hidden = wd.shape[1]
    hc = min(HC_MOE, hidden)
    assert hidden % hc == 0 and hc % LANES == 0
    w_in = lambda i, be, nu: (be[i], 0, 0)
    grid_spec = pltpu.PrefetchScalarGridSpec(
        num_scalar_prefetch=2,
        grid=(n_blocks,),
        in_specs=[
            pl.BlockSpec(memory_space=pl.ANY),
            pl.BlockSpec(memory_space=pl.ANY),
            pl.BlockSpec((None, d, 2 * hidden), w_in),
            pl.BlockSpec((None, 1, 2 * hidden), w_in),
            pl.BlockSpec((None, hidden, d), w_in),
            pl.BlockSpec((None, 1, d), w_in),
        ],
        out_specs=pl.BlockSpec((tm, d), lambda i, be, nu: (i, 0)),
        scratch_shapes=[
            pltpu.SMEM((2, tm), jnp.int32),
            pltpu.VMEM((2, tm, d), F32),
            pltpu.SemaphoreType.DMA((2,)),
            pltpu.SemaphoreType.DMA((2,)),
        ],
    )
    return pl.pallas_call(
        functools.partial(_moe_gemm_kernel, tm=tm, hidden=hidden, hc=hc),
        grid_spec=grid_spec,
        out_shape=jax.ShapeDtypeStruct((n_blocks * tm, d), F32),
        compiler_params=_cparams(("arbitrary",)),
        name="moe_gemm",
    )(block_expert, n_used, row_tok, x2d, wgu, bgu, wd, bd)


def _dispatch_tables(top_idx, tm, n_blocks):
    n = top_idx.shape[0]
    e_flat = top_idx.reshape(-1)
    onehot = (e_flat[:, None] == jnp.arange(N_EXPERTS, dtype=jnp.int32)[None, :]).astype(jnp.int32)
    csum = jnp.cumsum(onehot, axis=0)
    counts = csum[-1]
    rank = jnp.sum((csum - onehot) * onehot, axis=1)
    blocks_e = (counts + tm - 1) // tm
    bend = jnp.cumsum(blocks_e)
    bstart = bend - blocks_e
    dest = jnp.sum(onehot * bstart[None, :], axis=1) * tm + rank
    tok_flat = jnp.repeat(jnp.arange(n, dtype=jnp.int32), TOP_K)
    row_tok = jnp.zeros((n_blocks * tm,), jnp.int32).at[dest].set(tok_flat).reshape(n_blocks, tm)
    blk_ids = jnp.arange(n_blocks, dtype=jnp.int32)
    block_expert = jnp.minimum(jnp.sum((bend[None, :] <= blk_ids[:, None]).astype(jnp.int32), axis=1),
                               N_EXPERTS - 1).astype(jnp.int32)
    n_used = bend[-1:].astype(jnp.int32)
    return dest.reshape(n, TOP_K).astype(jnp.int32), row_tok, block_expert, n_used


def _combine_kernel(dest_hbm, yr_hbm, x_ref, tg_ref, g_ref, b_ref, o_ref, idx_smem, yg, isem, gsem,
                    *, tm, n_steps, alpha):
    i = pl.program_id(0)
    slot = i % 2

    def idx_copy(step, s):
        return pltpu.make_async_copy(dest_hbm.at[step], idx_smem.at[s], isem.at[s])

    def issue_gather(s):
        def body(r, carry):
            for k in range(TOP_K):
                src = idx_smem[s, k * tm + r]
                pltpu.make_async_copy(yr_hbm.at[pl.ds(src, 1)], yg.at[s, k, pl.ds(r, 1)], gsem.at[s]).start()
            return carry
        lax.fori_loop(0, tm, body, 0, unroll=8)

    def wait_gather(s):
        for k in range(TOP_K):
            pltpu.make_async_copy(yr_hbm.at[pl.ds(0, tm)], yg.at[s, k], gsem.at[s]).wait()

    @pl.when(i == 0)
    def _():
        idx_copy(0, 0).start()
        idx_copy(0, 0).wait()
        issue_gather(0)
        if n_steps > 1:
            idx_copy(1, 1).start()

    @pl.when(i + 1 < n_steps)
    def _():
        idx_copy(i + 1, 1 - slot).wait()
        issue_gather(1 - slot)

    @pl.when(i + 2 < n_steps)
    def _():
        idx_copy(i + 2, slot).start()

    wait_gather(slot)
    tg = tg_ref[...]
    y = tg[:, 0:1] * yg[slot, 0]
    for k in range(1, TOP_K):
        y = y + tg[:, k:k + 1] * yg[slot, k]
    o_ref[...] = _layer_norm_rows(alpha * x_ref[...] + y, g_ref[...], b_ref[...])


def _combine_ln(dest, yr, x1, gates, ln_g, ln_b, alpha):
    n, d = x1.shape
    tm = min(TM_COMBINE, n)
    assert n % tm == 0
    n_steps = n // tm
    dest_steps = dest.reshape(n_steps, tm, TOP_K).transpose(0, 2, 1).reshape(n_steps, TOP_K * tm)
    row = lambda i: (i, 0)
    fixed = lambda i: (0, 0)
    return pl.pallas_call(
        functools.partial(_combine_kernel, tm=tm, n_steps=n_steps, alpha=alpha),
        grid=(n_steps,),
        in_specs=[
            pl.BlockSpec(memory_space=pl.ANY),
            pl.BlockSpec(memory_space=pl.ANY),
            pl.BlockSpec((tm, d), row),
            pl.BlockSpec((tm, LANES), row),
            pl.BlockSpec((1, d), fixed),
            pl.BlockSpec((1, d), fixed),
        ],
        out_specs=pl.BlockSpec((tm, d), row),
        out_shape=jax.ShapeDtypeStruct((n, d), F32),
        scratch_shapes=[
            pltpu.SMEM((2, TOP_K * tm), jnp.int32),
            pltpu.VMEM((2, TOP_K, tm, d), F32),
            pltpu.SemaphoreType.DMA((2,)),
            pltpu.SemaphoreType.DMA((2,)),
        ],
        compiler_params=_cparams(("arbitrary",)),
        name="moe_combine_ln",
    )(dest_steps, yr, x1, gates, ln_g.reshape(1, d).astype(F32), ln_b.reshape(1, d).astype(F32))


def _moe_layer(x1, top_idx, gates, w_gate_up, b_gate_up, w_down, b_down, ln_g, ln_b, alpha):
    n, d = x1.shape
    tm = min(TM_MOE, n)
    n_blocks = -(-(n * TOP_K) // tm) + N_EXPERTS
    dest, row_tok, block_expert, n_used = _dispatch_tables(top_idx[:, :TOP_K], tm, n_blocks)
    hidden = w_down.shape[1]
    assert hidden % LANES == 0
    wgu = w_gate_up.astype(BF16)
    bgu = b_gate_up.reshape(N_EXPERTS, 1, 2 * hidden).astype(F32)
    half = LANES // 2
    wd = (w_down.reshape(N_EXPERTS, hidden // LANES, 2, half, d).transpose(0, 1, 3, 2, 4)
          .reshape(N_EXPERTS, hidden, d).astype(BF16))
    bd = b_down.reshape(N_EXPERTS, 1, d).astype(F32)
    yr = _moe_gemm(x1, row_tok, block_expert, n_used, wgu, bgu, wd, bd, tm)
    return _combine_ln(dest, yr, x1, gates, ln_g, ln_b, alpha)


def kernel(x, w_qkv_a, w_o_a, w_q_b, b_q_b, sinks_b, w_o_b, b_o_b, w_kv_shared, b_kv_shared, ln_mix_g, ln_mix_b, ln_ffn_g, ln_ffn_b, w_router, b_router, w_gate_up, b_gate_up, w_down, b_down):
    batch, seq, d = x.shape
    depth = ln_mix_g.shape[0]
    n_a = w_qkv_a.shape[0]
    alpha = (2 * depth) ** 0.25
    attn_w = N_HEADS * HEAD_DIM
    kv_a_w = N_KV_A * HEAD_DIM
    kv_b_w = N_KV_B * HEAD_DIM
    ga, gb = N_HEADS // N_KV_A, N_HEADS // N_KV_B
    rope = _rope_tables(seq)
    xs = x.reshape(batch * seq, d).astype(F32)
    kt_sh = v_sh = None
    for layer in range(depth):
        if layer < n_a:
            q, k, v = _project(xs, w_qkv_a[layer].astype(BF16), jnp.zeros((attn_w + 2 * kv_a_w,), F32), rope,
                               (attn_w, kv_a_w, kv_a_w), attn_w + kv_a_w, seq, tn=512)
            q_pad, k_hm, kta, va = _moba_operands(q, k, v, batch, seq, N_KV_A, ga)
            kmean = _kmean(k_hm, seq // MOBA_BLOCK)
            o_hm = _moba_attention(q_pad, kta, va, kmean)
            wo, bo = w_o_a[layer], jnp.zeros((d,), F32)
        else:
            jb = layer - n_a
            if jb == 0:
                k, v = _project(xs, w_kv_shared.astype(BF16), b_kv_shared, rope, (kv_b_w, kv_b_w), kv_b_w, seq,
                                tn=256)
                kt_sh = _to_heads(k, batch, seq, N_KV_B, 1)[:, :, 0].transpose(0, 1, 3, 2)
                v_sh = _to_heads(v, batch, seq, N_KV_B, 1)[:, :, 0]
            (q,) = _project(xs, w_q_b[jb].astype(BF16), b_q_b[jb], rope, (attn_w,), attn_w, seq, tn=512)
            o_hm = _swa_attention(_to_heads(q, batch, seq, N_KV_B, gb), kt_sh, v_sh, sinks_b[jb])
            wo, bo = w_o_b[jb], b_o_b[jb]
        x1, top_idx, gates = _oproj_ln_router(_from_heads(o_hm), xs, wo.astype(BF16), bo, ln_mix_g[layer],
                                              ln_mix_b[layer], w_router[layer], b_router[layer], alpha)
        xs = _moe_layer(x1, top_idx, gates, w_gate_up[layer], b_gate_up[layer], w_down[layer], b_down[layer],
                        ln_ffn_g[layer], ln_ffn_b[layer], alpha)
    return xs.reshape(batch, seq, d).astype(x.dtype)
```

```python
import functools

import numpy as np
import jax
import jax.numpy as jnp
from jax import lax
from jax.experimental import pallas as pl
from jax.experimental.pallas import tpu as pltpu

F32 = jnp.float32
BF16 = jnp.bfloat16
NEG_INF = float("-inf")

HEAD_DIM = 64
N_HEADS = 32
N_KV_A = 8
N_KV_B = 4
ROPE_DIM = 16
ROPE_THETA = 500000.0
MOBA_BLOCK = 256
MOBA_TOPK = 3
SWA_WINDOW = 128
N_EXPERTS = 32
TOP_K = 4
SWIGLU_LIMIT = 7.0
SWIGLU_ALPHA = 1.702
LN_EPS = 1e-5

MOBA_MASK_BIAS = -(2.0 ** 100)

LANES = 128
SUBLANES = 8
V7X_VMEM_BYTES = 64 * 1024 * 1024
VMEM_LIMIT = V7X_VMEM_BYTES - 8 * 1024 * 1024

MOBA_KV_TILE = 512
TM_PROJ = 512
TM_OPROJ = 256
TM_MOE = 512
TM_COMBINE = 256
HC_MOE = 512


def _cparams(semantics):
    return pltpu.CompilerParams(dimension_semantics=semantics, vmem_limit_bytes=VMEM_LIMIT)


def _proj_kernel(x_ref, w_ref, b_ref, c_ref, sa_ref, sb_ref, *o_refs, splits, rope_cols, tn):
    x = x_ref[...].astype(BF16)
    col = 0
    for o_ref, width in zip(o_refs, splits):
        for c0 in range(0, width, tn):
            lo = col + c0
            y = jnp.dot(x, w_ref[:, lo:lo + tn], preferred_element_type=F32) + b_ref[:, lo:lo + tn]
            if lo < rope_cols:
                for l0 in range(0, tn, LANES):
                    seg = y[:, l0:l0 + LANES]
                    rot = (seg * c_ref[...] + pltpu.roll(seg, LANES - ROPE_DIM // 2, 1) * sa_ref[...]
                           + pltpu.roll(seg, ROPE_DIM // 2, 1) * sb_ref[...])
                    o_ref[:, c0 + l0:c0 + l0 + LANES] = rot.astype(o_ref.dtype)
            else:
                o_ref[:, c0:c0 + tn] = y.astype(o_ref.dtype)
        col += width


def _project(x2d, w_bf16, bias, rope_tabs, splits, rope_cols, seq, tn):
    n, d = x2d.shape
    nout = w_bf16.shape[1]
    tm = min(TM_PROJ, seq)
    assert n % tm == 0 and seq % tm == 0 and sum(splits) == nout
    assert all(s % tn == 0 for s in splits) and rope_cols % tn == 0 and tn % LANES == 0
    pos_blocks = seq // tm
    c_tab, sa_tab, sb_tab = rope_tabs
    tab_spec = pl.BlockSpec((tm, LANES), lambda i: (i % pos_blocks, 0))
    kern = functools.partial(_proj_kernel, splits=tuple(splits), rope_cols=rope_cols, tn=tn)
    return pl.pallas_call(
        kern,
        grid=(n // tm,),
        in_specs=[
            pl.BlockSpec((tm, d), lambda i: (i, 0)),
            pl.BlockSpec((d, nout), lambda i: (0, 0)),
            pl.BlockSpec((1, nout), lambda i: (0, 0)),
            tab_spec, tab_spec, tab_spec,
        ],
        out_specs=[pl.BlockSpec((tm, s), lambda i: (i, 0)) for s in splits],
        out_shape=[jax.ShapeDtypeStruct((n, s), BF16) for s in splits],
        compiler_params=_cparams(("parallel",)),
        name="proj_rope",
    )(x2d, w_bf16, bias.reshape(1, nout).astype(F32), c_tab, sa_tab, sb_tab)


def _rope_tables(seq):
    half = ROPE_DIM // 2
    inv = ROPE_THETA ** (-jnp.arange(0, ROPE_DIM, 2, dtype=F32) / ROPE_DIM)
    ang = jnp.arange(seq, dtype=F32)[:, None] * inv[None, :]
    cos, sin = jnp.cos(ang), jnp.sin(ang)
    ones = jnp.ones((seq, HEAD_DIM - ROPE_DIM), F32)
    zeros = jnp.zeros((seq, HEAD_DIM - ROPE_DIM), F32)
    zh = jnp.zeros((seq, half), F32)
    c_head = jnp.concatenate([cos, cos, ones], axis=1)
    sa_head = jnp.concatenate([-sin, zh, zeros], axis=1)
    sb_head = jnp.concatenate([zh, sin, zeros], axis=1)
    rep = LANES // HEAD_DIM
    return tuple(jnp.tile(t, (1, rep)) for t in (c_head, sa_head, sb_head))


def _to_heads(t2d, batch, seq, hkv, groups):
    return t2d.reshape(batch, seq, hkv, groups, HEAD_DIM).transpose(0, 2, 3, 1, 4)


def _from_heads(t_hm):
    b, hkv, groups, s, _ = t_hm.shape
    return t_hm.transpose(0, 3, 1, 2, 4).reshape(b * s, hkv * groups * HEAD_DIM)


def _kmean_kernel(k_ref, o_ref, *, nb, blk):
    k = k_ref[...].astype(F32)
    km = jnp.mean(k.reshape(nb, blk, HEAD_DIM), axis=1)
    o_ref[...] = jnp.zeros(o_ref.shape, F32)
    o_ref[0:nb, 0:HEAD_DIM] = km


def _kmean(k_hm, nb):
    b, hkv, s, _ = k_hm.shape
    assert nb <= LANES
    return pl.pallas_call(
        functools.partial(_kmean_kernel, nb=nb, blk=MOBA_BLOCK),
        grid=(b, hkv),
        in_specs=[pl.BlockSpec((None, None, s, HEAD_DIM), lambda i, j: (i, j, 0, 0))],
        out_specs=pl.BlockSpec((None, None, LANES, LANES), lambda i, j: (i, j, 0, 0)),
        out_shape=jax.ShapeDtypeStruct((b, hkv, LANES, LANES), F32),
        compiler_params=_cparams(("parallel", "parallel")),
        name="moba_kmean",
    )(k_hm)


def _moba_kernel(it_ref, jt_ref, q_ref, kta_ref, va_ref, km_ref, o_ref, qa_scr, m_scr, acc_scr,
                 *, groups, blk, kv_tile, nbp, scale):
    p = pl.program_id(2)
    i = it_ref[p]
    jt = jt_ref[p]
    own_tile = i // (kv_tile // blk)
    is_diag = jt == own_tile
    rows = groups * blk

    def softmax_step(s, first):
        parts = [s[:, c:c + LANES] for c in range(0, kv_tile, LANES)]
        mx = parts[0]
        for part in parts[1:]:
            mx = jnp.maximum(mx, part)
        rm = jnp.max(mx, axis=-1, keepdims=True)
        if first:
            m_new = jnp.broadcast_to(rm, (rows, LANES))
        else:
            m_old = m_scr[...]
            m_new = jnp.maximum(m_old, rm)
        pr = jnp.concatenate([jnp.exp(part - m_new) for part in parts], axis=1).astype(BF16)
        pv = jnp.dot(pr, va_ref[...], preferred_element_type=F32)
        if first:
            acc_scr[...] = pv
        else:
            acc_scr[...] = jnp.exp(m_old - m_new) * acc_scr[...] + pv
        m_scr[...] = m_new

    @pl.when(is_diag)
    def _():
        qf = q_ref[...].reshape(rows, LANES).astype(F32)
        gate = lax.dot_general(km_ref[0:nbp, :], qf, (((1,), (1,)), ((), ())),
                               precision=lax.Precision.HIGHEST, preferred_element_type=F32)
        kb = lax.broadcasted_iota(jnp.int32, (nbp, rows), 0)
        kb_f = kb.astype(F32)
        cur = jnp.where(kb < i, gate, NEG_INF)
        sel = jnp.zeros((nbp, rows), F32)
        for _ in range(MOBA_TOPK):
            mx = jnp.max(cur, axis=0, keepdims=True)
            idx = jnp.min(jnp.where(cur == mx, kb_f, float(nbp)), axis=0, keepdims=True)
            hit = kb_f == idx
            sel = jnp.where(hit & (mx > NEG_INF), 1.0, sel)
            cur = jnp.where(hit, NEG_INF, cur)
        bias_t = jnp.where((sel > 0.0) | (kb >= i), 0.0, MOBA_MASK_BIAS)
        bias_t = jnp.concatenate([jnp.zeros((HEAD_DIM, rows), F32), bias_t,
                                  jnp.zeros((LANES - HEAD_DIM - nbp, rows), F32)], axis=0)
        qa = (qf * scale + bias_t.T).astype(BF16)
        qa_scr[...] = qa
        s = jnp.dot(qa, kta_ref[...], preferred_element_type=F32)
        qpos = i * blk + lax.broadcasted_iota(jnp.int32, (groups, blk, kv_tile), 1).reshape(rows, kv_tile)
        kpos = jt * kv_tile + lax.broadcasted_iota(jnp.int32, (rows, kv_tile), 1)
        s = jnp.where(kpos <= qpos, s, NEG_INF)
        softmax_step(s, True)

    @pl.when(jnp.logical_not(is_diag))
    def _():
        s = jnp.dot(qa_scr[...], kta_ref[...], preferred_element_type=F32)
        softmax_step(s, False)

    @pl.when((jt == own_tile - 1) | (own_tile == 0))
    def _():
        acc = acc_scr[...]
        o = (acc[:, :HEAD_DIM] / acc[:, HEAD_DIM:HEAD_DIM + 1]).astype(o_ref.dtype)
        o_ref[...] = o.reshape(groups, blk, HEAD_DIM)


def _moba_attention(q_pad, kta, va, kmean):
    b, hkv, groups, s, _ = q_pad.shape
    blk = MOBA_BLOCK
    assert s % blk == 0 and blk == 2 * LANES
    nb = s // blk
    assert nb <= LANES - HEAD_DIM
    kv_tile = MOBA_KV_TILE if s % MOBA_KV_TILE == 0 else blk
    per_tile = kv_tile // blk
    it, jt = [], []
    for i in range(nb):
        it.append(i)
        jt.append(i // per_tile)
        for j in range(i // per_tile):
            it.append(i)
            jt.append(j)
    it = jnp.asarray(np.asarray(it, np.int32))
    jt = jnp.asarray(np.asarray(jt, np.int32))
    nbp = -(-nb // SUBLANES) * SUBLANES
    assert nbp < LANES - HEAD_DIM
    kern = functools.partial(_moba_kernel, groups=groups, blk=blk, kv_tile=kv_tile, nbp=nbp,
                             scale=HEAD_DIM ** -0.5)
    grid_spec = pltpu.PrefetchScalarGridSpec(
        num_scalar_prefetch=2,
        grid=(b, hkv, int(it.shape[0])),
        in_specs=[
            pl.BlockSpec((None, None, groups, blk, LANES), lambda bi, h, p, it, jt: (bi, h, 0, it[p], 0)),
            pl.BlockSpec((None, None, LANES, kv_tile), lambda bi, h, p, it, jt: (bi, h, 0, jt[p])),
            pl.BlockSpec((None, None, kv_tile, LANES), lambda bi, h, p, it, jt: (bi, h, jt[p], 0)),
            pl.BlockSpec((None, None, LANES, LANES), lambda bi, h, p, it, jt: (bi, h, 0, 0)),
        ],
        out_specs=pl.BlockSpec((None, None, groups, blk, HEAD_DIM), lambda bi, h, p, it, jt: (bi, h, 0, it[p], 0)),
        scratch_shapes=[
            pltpu.VMEM((groups * blk, LANES), BF16),
            pltpu.VMEM((groups * blk, LANES), F32),
            pltpu.VMEM((groups * blk, LANES), F32),
        ],
    )
    return pl.pallas_call(
        kern,
        grid_spec=grid_spec,
        out_shape=jax.ShapeDtypeStruct((b, hkv, groups, s, HEAD_DIM), BF16),
        compiler_params=_cparams(("parallel", "parallel", "arbitrary")),
        name="moba_attn",
    )(it, jt, q_pad, kta, va, kmean)


def _moba_operands(q2d, k2d, v2d, batch, seq, hkv, groups):
    q_hm = _to_heads(q2d, batch, seq, hkv, groups)
    q_pad = jnp.concatenate([q_hm, jnp.zeros_like(q_hm)], axis=-1)
    k_hm = _to_heads(k2d, batch, seq, hkv, 1)[:, :, 0]
    v_hm = _to_heads(v2d, batch, seq, hkv, 1)[:, :, 0]
    blk_of_key = jnp.arange(seq, dtype=jnp.int32) // MOBA_BLOCK
    onehot = (jnp.arange(LANES - HEAD_DIM, dtype=jnp.int32)[:, None] == blk_of_key[None, :]).astype(BF16)
    kta = jnp.concatenate([k_hm.transpose(0, 1, 3, 2),
                           jnp.broadcast_to(onehot, (batch, hkv) + onehot.shape)], axis=2)
    ones = jnp.ones(v_hm.shape[:-1] + (1,), BF16)
    zeros = jnp.zeros(v_hm.shape[:-1] + (LANES - HEAD_DIM - 1,), BF16)
    va = jnp.concatenate([v_hm, ones, zeros], axis=-1)
    return q_pad, k_hm, kta, va


def _swa_kernel(q_ref, ktp_ref, ktc_ref, vp_ref, vc_ref, sink_ref, o_ref, *, groups, win, scale):
    kh = pl.program_id(1)
    i = pl.program_id(2)
    row = lax.broadcasted_iota(jnp.int32, (win, win), 0)
    col = lax.broadcasted_iota(jnp.int32, (win, win), 1)
    cur_mask = col <= row
    prev_mask = (col > row) & (i > 0)
    for g in range(groups):
        qg = q_ref[g]
        sp = jnp.dot(qg, ktp_ref[...], preferred_element_type=F32) * scale
        sc = jnp.dot(qg, ktc_ref[...], preferred_element_type=F32) * scale
        sp = jnp.where(prev_mask, sp, NEG_INF)
        sc = jnp.where(cur_mask, sc, NEG_INF)
        sink = sink_ref[kh * groups + g]
        mx = jnp.maximum(jnp.maximum(jnp.max(sp, axis=-1, keepdims=True), jnp.max(sc, axis=-1, keepdims=True)), sink)
        pp = jnp.exp(sp - mx)
        pc = jnp.exp(sc - mx)
        den = jnp.sum(pp, axis=-1, keepdims=True) + jnp.sum(pc, axis=-1, keepdims=True) + jnp.exp(sink - mx)
        o = (jnp.dot(pp.astype(BF16), vp_ref[...], preferred_element_type=F32)
             + jnp.dot(pc.astype(BF16), vc_ref[...], preferred_element_type=F32))
        o_ref[g] = (o / den).astype(o_ref.dtype)


def _swa_attention(q_hm, kt_hm, v_hm, sinks):
    b, hkv, groups, s, _ = q_hm.shape
    win = SWA_WINDOW
    assert s % win == 0
    nb = s // win
    kern = functools.partial(_swa_kernel, groups=groups, win=win, scale=HEAD_DIM ** -0.5)
    return pl.pallas_call(
        kern,
        grid=(b, hkv, nb),
        in_specs=[
            pl.BlockSpec((None, None, groups, win, HEAD_DIM), lambda bi, h, i: (bi, h, 0, i, 0)),
            pl.BlockSpec((None, None, HEAD_DIM, win), lambda bi, h, i: (bi, h, 0, jnp.maximum(i - 1, 0))),
            pl.BlockSpec((None, None, HEAD_DIM, win), lambda bi, h, i: (bi, h, 0, i)),
            pl.BlockSpec((None, None, win, HEAD_DIM), lambda bi, h, i: (bi, h, jnp.maximum(i - 1, 0), 0)),
            pl.BlockSpec((None, None, win, HEAD_DIM), lambda bi, h, i: (bi, h, i, 0)),
            pl.BlockSpec(memory_space=pltpu.SMEM),
        ],
        out_specs=pl.BlockSpec((None, None, groups, win, HEAD_DIM), lambda bi, h, i: (bi, h, 0, i, 0)),
        out_shape=jax.ShapeDtypeStruct(q_hm.shape, BF16),
        compiler_params=_cparams(("parallel", "parallel", "parallel")),
        name="swa_attn",
    )(q_hm, kt_hm, kt_hm, v_hm, v_hm, sinks.astype(F32))


def _layer_norm_rows(z, g, b):
    mu = jnp.mean(z, axis=-1, keepdims=True)
    zc = z - mu
    var = jnp.mean(zc * zc, axis=-1, keepdims=True)
    return zc * lax.rsqrt(var + LN_EPS) * g + b


def _oproj_kernel(a_ref, x_ref, wo_ref, bo_ref, g_ref, b_ref, wr_ref, br_ref, x1_ref, ti_ref, tg_ref,
                  *, alpha, n_experts):
    mix = jnp.dot(a_ref[...], wo_ref[...], preferred_element_type=F32) + bo_ref[...]
    x1 = _layer_norm_rows(alpha * x_ref[...] + mix, g_ref[...], b_ref[...])
    x1_ref[...] = x1
    logits = jnp.dot(x1, wr_ref[...], precision=lax.Precision.HIGHEST, preferred_element_type=F32) + br_ref[...]
    lane = lax.broadcasted_iota(jnp.int32, logits.shape, 1)
    cur = jnp.where(lane < n_experts, logits, NEG_INF)
    vals, idxs = [], []
    for _ in range(TOP_K):
        mx = jnp.max(cur, axis=-1, keepdims=True)
        idx = jnp.min(jnp.where(cur == mx, lane, LANES), axis=-1, keepdims=True)
        vals.append(mx)
        idxs.append(idx)
        cur = jnp.where(lane == idx, NEG_INF, cur)
    exps = [jnp.exp(v - vals[0]) for v in vals]
    den = exps[0]
    for e in exps[1:]:
        den = den + e
    ti = jnp.zeros(logits.shape, jnp.int32)
    tg = jnp.zeros(logits.shape, F32)
    for k in range(TOP_K):
        ti = jnp.where(lane == k, idxs[k], ti)
        tg = jnp.where(lane == k, exps[k] / den, tg)
    ti_ref[...] = ti
    tg_ref[...] = tg


def _oproj_ln_router(attn2d, x2d, wo_bf16, bo, ln_g, ln_b, w_r, b_r, alpha):
    n, d = x2d.shape
    da = attn2d.shape[1]
    tm = min(TM_OPROJ, n)
    assert n % tm == 0
    n_experts = w_r.shape[1]
    wr_pad = jnp.zeros((d, LANES), F32).at[:, :n_experts].set(w_r.astype(F32))
    br_pad = jnp.zeros((1, LANES), F32).at[0, :n_experts].set(b_r.astype(F32))
    row = lambda i: (i, 0)
    fixed = lambda i: (0, 0)
    return pl.pallas_call(
        functools.partial(_oproj_kernel, alpha=alpha, n_experts=n_experts),
        grid=(n // tm,),
        in_specs=[
            pl.BlockSpec((tm, da), row),
            pl.BlockSpec((tm, d), row),
            pl.BlockSpec((da, d), fixed),
            pl.BlockSpec((1, d), fixed),
            pl.BlockSpec((1, d), fixed),
            pl.BlockSpec((1, d), fixed),
            pl.BlockSpec((d, LANES), fixed),
            pl.BlockSpec((1, LANES), fixed),
        ],
        out_specs=[pl.BlockSpec((tm, d), row), pl.BlockSpec((tm, LANES), row), pl.BlockSpec((tm, LANES), row)],
        out_shape=[jax.ShapeDtypeStruct((n, d), F32), jax.ShapeDtypeStruct((n, LANES), jnp.int32),
                   jax.ShapeDtypeStruct((n, LANES), F32)],
        compiler_params=_cparams(("parallel",)),
        name="oproj_ln_router",
    )(attn2d, x2d, wo_bf16, bo.reshape(1, d).astype(F32), ln_g.reshape(1, d).astype(F32),
      ln_b.reshape(1, d).astype(F32), wr_pad, br_pad)


def _moe_gemm_kernel(be_ref, nu_ref, src_hbm, dst_hbm, x_hbm, wgu_ref, bgu_ref, wd_ref, bd_ref, y_hbm,
                     gtab, stab, xg0, xg1, yb0, yb1, gtsem, stsem, gsem, ssem, *, tm, hidden, hc, n_blocks):
    del be_ref
    i = pl.program_id(0)
    n_used = nu_ref[0]
    last = n_used - 1
    xg = (xg0, xg1)
    yb = (yb0, yb1)

    def src_copy(blk, s):
        return pltpu.make_async_copy(src_hbm.at[blk], gtab.at[pl.ds(s, 1)], gtsem.at[s])

    def dst_copy(blk, s):
        return pltpu.make_async_copy(dst_hbm.at[blk], stab.at[pl.ds(s, 1)], stsem.at[s])

    def gather_start(s, r):
        tok = gtab[s, r]
        pltpu.make_async_copy(x_hbm.at[pl.ds(tok, 1)], xg[s].at[pl.ds(r, 1)], gsem.at[s]).start()

    def gather_wait(s):
        pltpu.make_async_copy(x_hbm.at[pl.ds(0, tm)], xg[s], gsem.at[s]).wait()

    def scatter_start(s, r):
        row = stab[s, r]
        pltpu.make_async_copy(yb[s].at[pl.ds(r, 1)], y_hbm.at[pl.ds(row, 1)], ssem.at[s]).start()

    def scatter_wait(s):
        pltpu.make_async_copy(yb[s], y_hbm.at[pl.ds(0, tm)], ssem.at[s]).wait()

    @pl.when(i == 0)
    def _():
        src_copy(0, 0).start()
        src_copy(jnp.minimum(1, last), 1).start()
        dst_copy(n_blocks, 1).start()
        src_copy(0, 0).wait()

        def body(r, carry):
            gather_start(0, r)
            return carry
        lax.fori_loop(0, tm, body, 0, unroll=8)
        yb1[...] = jnp.zeros(yb1.shape, F32)
        tail = y_hbm.shape[0] - 2 * tm
        for half in range(2):
            pltpu.make_async_copy(yb1, y_hbm.at[pl.ds(tail + half * tm, tm)], ssem.at[1]).start()
        for half in range(2):
            pltpu.make_async_copy(yb1, y_hbm.at[pl.ds(tail + half * tm, tm)], ssem.at[1]).wait()

    def step(slot):
        nslot = 1 - slot
        src_copy(jnp.minimum(i + 2, last), slot).start()
        dst_copy(i, slot).start()
        src_copy(0, nslot).wait()
        dst_copy(0, nslot).wait()
        gather_wait(slot)

        @pl.when(i >= 1)
        def _():
            scatter_wait(slot)

        xb = xg[slot][...].astype(BF16)
        even = lax.broadcasted_iota(jnp.int32, (tm, LANES), 1) % 2 == 0
        n_chunks = hidden // hc
        rows_per_chunk = tm // n_chunks
        for ci in range(n_chunks):
            c = ci * hc
            for r in range(ci * rows_per_chunk, (ci + 1) * rows_per_chunk):
                gather_start(nslot, r)
                scatter_start(nslot, r)
            h = (jnp.dot(xb, wgu_ref[:, 2 * c:2 * (c + hc)], preferred_element_type=F32)
                 + bgu_ref[:, 2 * c:2 * (c + hc)])
            pieces = []
            for a in range(hc // LANES):
                acts = []
                for half in range(2):
                    hv = h[:, (2 * a + half) * LANES:(2 * a + half + 1) * LANES]
                    glu = jnp.minimum(hv, SWIGLU_LIMIT)
                    lin = jnp.clip(pltpu.roll(hv, LANES - 1, 1), -SWIGLU_LIMIT, SWIGLU_LIMIT)
                    acts.append(glu * jax.nn.sigmoid(SWIGLU_ALPHA * glu) * (lin + 1.0))
                pieces.append(jnp.where(even, acts[0], pltpu.roll(acts[1], 1, 1)).astype(BF16))
            act = jnp.concatenate(pieces, axis=1) if len(pieces) > 1 else pieces[0]
            part = jnp.dot(act, wd_ref[c:c + hc, :], preferred_element_type=F32)
            if ci == 0:
                yb[slot][...] = part + bd_ref[...]
            else:
                yb[slot][...] += part

    def drain(slot):
        nslot = 1 - slot
        src_copy(0, nslot).wait()
        dst_copy(0, nslot).wait()

        def body(r, carry):
            scatter_start(nslot, r)
            return carry
        lax.fori_loop(0, tm, body, 0, unroll=8)
        scatter_wait(slot)
        scatter_wait(nslot)
        gather_wait(slot)

    for parity in range(2):
        @pl.when((i < n_used) & (i % 2 == parity))
        def _():
            step(parity)

        @pl.when((i == n_used) & (i % 2 == parity))
        def _():
            drain(parity)


def _moe_gemm(x2d, src_tab, dst_tab, block_expert, n_used, wgu, bgu, wd, bd, tm):
    n, d = x2d.shape
    n_blocks = src_tab.shape[0]
    hidden = wd.shape[1]
    hc = min(HC_MOE, hidden)
    assert hidden % hc == 0 and hc % LANES == 0 and tm % (hidden // hc) == 0
    assert dst_tab.shape[0] == n_blocks + 1 and block_expert.shape[0] == n_blocks + 1
    w_in = lambda i, be, nu: (be[i], 0, 0)
    grid_spec = pltpu.PrefetchScalarGridSpec(
        num_scalar_prefetch=2,
        grid=(n_blocks + 1,),
        in_specs=[
            pl.BlockSpec(memory_space=pl.ANY),
            pl.BlockSpec(memory_space=pl.ANY),
            pl.BlockSpec(memory_space=pl.ANY),
            pl.BlockSpec((None, d, 2 * hidden), w_in),
            pl.BlockSpec((None, 1, 2 * hidden), w_in),
            pl.BlockSpec((None, hidden, d), w_in),
            pl.BlockSpec((None, 1, d), w_in),
        ],
        out_specs=pl.BlockSpec(memory_space=pl.ANY),
        scratch_shapes=[
            pltpu.SMEM((2, tm), jnp.int32),
            pltpu.SMEM((2, tm), jnp.int32),
            pltpu.VMEM((tm, d), F32),
            pltpu.VMEM((tm, d), F32),
            pltpu.VMEM((tm, d), F32),
            pltpu.VMEM((tm, d), F32),
            pltpu.SemaphoreType.DMA((2,)),
            pltpu.SemaphoreType.DMA((2,)),
            pltpu.SemaphoreType.DMA((2,)),
            pltpu.SemaphoreType.DMA((2,)),
        ],
    )
    y_rows = n * TOP_K + 2 * tm
    return pl.pallas_call(
        functools.partial(_moe_gemm_kernel, tm=tm, hidden=hidden, hc=hc, n_blocks=n_blocks),
        grid_spec=grid_spec,
        out_shape=jax.ShapeDtypeStruct((y_rows, d), F32),
        compiler_params=_cparams(("arbitrary",)),
        name="moe_gemm",
    )(block_expert, n_used, src_tab, dst_tab, x2d, wgu, bgu, wd, bd)


def _dispatch_tables(top_idx, tm, n_blocks):
    n = top_idx.shape[0]
    nk = n * TOP_K
    e_flat = top_idx.reshape(-1)
    onehot = (e_flat[:, None] == jnp.arange(N_EXPERTS, dtype=jnp.int32)[None, :]).astype(jnp.int32)
    csum = jnp.cumsum(onehot, axis=0)
    counts = csum[-1]
    rank = jnp.sum((csum - onehot) * onehot, axis=1)
    blocks_e = (counts + tm - 1) // tm
    bend = jnp.cumsum(blocks_e)
    bstart = bend - blocks_e
    dest = jnp.sum(onehot * bstart[None, :], axis=1) * tm + rank
    row_pair = jnp.full((n_blocks * tm,), -1, jnp.int32).at[dest].set(jnp.arange(nk, dtype=jnp.int32))
    row_pair = row_pair.reshape(n_blocks, tm)
    blk_ids = jnp.arange(n_blocks + 1, dtype=jnp.int32)
    trash = nk + (blk_ids[:n_blocks, None] % 2) * tm + jnp.arange(tm, dtype=jnp.int32)[None, :]
    src_tab = jnp.where(row_pair >= 0, row_pair // TOP_K, 0).astype(jnp.int32).reshape(n_blocks, 1, tm)
    dst_tab = jnp.where(row_pair >= 0, (row_pair % TOP_K) * n + row_pair // TOP_K, trash)
    before_first = nk + tm + jnp.arange(tm, dtype=jnp.int32)[None, :]
    dst_tab = jnp.concatenate([dst_tab, before_first], axis=0).astype(jnp.int32).reshape(n_blocks + 1, 1, tm)
    block_expert = jnp.minimum(jnp.sum((bend[None, :] <= blk_ids[:, None]).astype(jnp.int32), axis=1),
                               N_EXPERTS - 1).astype(jnp.int32)
    n_used = bend[-1:].astype(jnp.int32)
    return src_tab, dst_tab, block_expert, n_used


def _combine_kernel(*refs, alpha):
    y_refs, (x_ref, tg_ref, g_ref, b_ref, o_ref) = refs[:TOP_K], refs[TOP_K:]
    tg = tg_ref[...]
    y = tg[:, 0:1] * y_refs[0][...]
    for k in range(1, TOP_K):
        y = y + tg[:, k:k + 1] * y_refs[k][...]
    o_ref[...] = _layer_norm_rows(alpha * x_ref[...] + y, g_ref[...], b_ref[...])


def _combine_ln(y4, x1, gates, ln_g, ln_b, alpha):
    n, d = x1.shape
    tm = min(TM_COMBINE, n)
    assert n % tm == 0
    row = lambda i: (i, 0)
    fixed = lambda i: (0, 0)
    return pl.pallas_call(
        functools.partial(_combine_kernel, alpha=alpha),
        grid=(n // tm,),
        in_specs=[pl.BlockSpec((tm, d), functools.partial(lambda k, i: (k * (n // tm) + i, 0), k))
                  for k in range(TOP_K)] + [
            pl.BlockSpec((tm, d), row),
            pl.BlockSpec((tm, LANES), row),
            pl.BlockSpec((1, d), fixed),
            pl.BlockSpec((1, d), fixed),
        ],
        out_specs=pl.BlockSpec((tm, d), row),
        out_shape=jax.ShapeDtypeStruct((n, d), F32),
        compiler_params=_cparams(("parallel",)),
        name="moe_combine_ln",
    )(*([y4] * TOP_K), x1, gates, ln_g.reshape(1, d).astype(F32), ln_b.reshape(1, d).astype(F32))


def _moe_layer(x1, top_idx, gates, w_gate_up, b_gate_up, w_down, b_down, ln_g, ln_b, alpha):
    n, d = x1.shape
    tm = min(TM_MOE, n)
    n_blocks = -(-(n * TOP_K) // tm) + N_EXPERTS
    src_tab, dst_tab, block_expert, n_used = _dispatch_tables(top_idx[:, :TOP_K], tm, n_blocks)
    hidden = w_down.shape[1]
    assert hidden % LANES == 0
    wgu = w_gate_up.astype(BF16)
    bgu = b_gate_up.reshape(N_EXPERTS, 1, 2 * hidden).astype(F32)
    half = LANES // 2
    wd = (w_down.reshape(N_EXPERTS, hidden // LANES, 2, half, d).transpose(0, 1, 3, 2, 4)
          .reshape(N_EXPERTS, hidden, d).astype(BF16))
    bd = b_down.reshape(N_EXPERTS, 1, d).astype(F32)
    y4 = _moe_gemm(x1, src_tab, dst_tab, block_expert, n_used, wgu, bgu, wd, bd, tm)
    return _combine_ln(y4, x1, gates, ln_g, ln_b, alpha)


def kernel(x, w_qkv_a, w_o_a, w_q_b, b_q_b, sinks_b, w_o_b, b_o_b, w_kv_shared, b_kv_shared, ln_mix_g, ln_mix_b, ln_ffn_g, ln_ffn_b, w_router, b_router, w_gate_up, b_gate_up, w_down, b_down):
    batch, seq, d = x.shape
    depth = ln_mix_g.shape[0]
    n_a = w_qkv_a.shape[0]
    alpha = (2 * depth) ** 0.25
    attn_w = N_HEADS * HEAD_DIM
    kv_a_w = N_KV_A * HEAD_DIM
    kv_b_w = N_KV_B * HEAD_DIM
    ga, gb = N_HEADS // N_KV_A, N_HEADS // N_KV_B
    rope = _rope_tables(seq)
    xs = x.reshape(batch * seq, d).astype(F32)
    kt_sh = v_sh = None
    for layer in range(depth):
        if layer < n_a:
            q, k, v = _project(xs, w_qkv_a[layer].astype(BF16), jnp.zeros((attn_w + 2 * kv_a_w,), F32), rope,
                               (attn_w, kv_a_w, kv_a_w), attn_w + kv_a_w, seq, tn=512)
            q_pad, k_hm, kta, va = _moba_operands(q, k, v, batch, seq, N_KV_A, ga)
            kmean = _kmean(k_hm, seq // MOBA_BLOCK)
            attn2d = _from_heads(_moba_attention(q_pad, kta, va, kmean))
            wo, bo = w_o_a[layer], jnp.zeros((d,), F32)
        else:
            jb = layer - n_a
            if jb == 0:
                k, v = _project(xs, w_kv_shared.astype(BF16), b_kv_shared, rope, (kv_b_w, kv_b_w), kv_b_w, seq,
                                tn=256)
                kt_sh = _to_heads(k, batch, seq, N_KV_B, 1)[:, :, 0].transpose(0, 1, 3, 2)
                v_sh = _to_heads(v, batch, seq, N_KV_B, 1)[:, :, 0]
            (q,) = _project(xs, w_q_b[jb].astype(BF16), b_q_b[jb], rope, (attn_w,), attn_w, seq, tn=512)
            attn2d = _from_heads(_swa_attention(_to_heads(q, batch, seq, N_KV_B, gb), kt_sh, v_sh, sinks_b[jb]))
            wo, bo = w_o_b[jb], b_o_b[jb]
        x1, top_idx, gates = _oproj_ln_router(attn2d, xs, wo.astype(BF16), bo, ln_mix_g[layer],
                                              ln_mix_b[layer], w_router[layer], b_router[layer], alpha)
        xs = _moe_layer(x1, top_idx, gates, w_gate_up[layer], b_gate_up[layer], w_down[layer], b_down[layer],
                        ln_ffn_g[layer], ln_ffn_b[layer], alpha)
    return xs.reshape(batch, seq, d).astype(x.dtype)
```

```python
import functools

import numpy as np
import jax
import jax.numpy as jnp
from jax import lax
from jax.experimental import pallas as pl
from jax.experimental.pallas import tpu as pltpu

F32 = jnp.float32
BF16 = jnp.bfloat16
NEG_INF = float("-inf")

HEAD_DIM = 64
N_HEADS = 32
N_KV_A = 8
N_KV_B = 4
ROPE_DIM = 16
ROPE_THETA = 500000.0
MOBA_BLOCK = 256
MOBA_TOPK = 3
SWA_WINDOW = 128
N_EXPERTS = 32
TOP_K = 4
SWIGLU_LIMIT = 7.0
SWIGLU_ALPHA = 1.702
LN_EPS = 1e-5

MOBA_MASK_BIAS = -(2.0 ** 100)

LANES = 128
SUBLANES = 8
V7X_VMEM_BYTES = 64 * 1024 * 1024
VMEM_LIMIT = V7X_VMEM_BYTES - 8 * 1024 * 1024

MOBA_KV_TILE = 512
TM_PROJ = 512
TM_OPROJ = 256
TM_MOE = 512
TM_COMBINE = 256
HC_MOE = 256


def _cparams(semantics):
    return pltpu.CompilerParams(dimension_semantics=semantics, vmem_limit_bytes=VMEM_LIMIT)


def _proj_kernel(x_ref, w_ref, b_ref, c_ref, sa_ref, sb_ref, *o_refs, splits, modes, rope_cols, tn, tm, pos_blocks):
    x = x_ref[...].astype(BF16)
    lane = lax.broadcasted_iota(jnp.int32, (tm, LANES), 1)
    lower = lane < HEAD_DIM
    fills = {}
    if 'zero' in modes:
        fills['zero'] = jnp.zeros((tm, LANES), F32)
    if 'one' in modes:
        fills['one'] = jnp.where(lane == HEAD_DIM, 1.0, 0.0)
    if 'block' in modes:
        pos = (pl.program_id(0) % pos_blocks) * tm + lax.broadcasted_iota(jnp.int32, (tm, LANES), 0)
        fills['block'] = jnp.where(lane - HEAD_DIM == pos // MOBA_BLOCK, 1.0, 0.0)
    col = 0
    for o_ref, width, mode in zip(o_refs, splits, modes):
        for c0 in range(0, width, tn):
            lo = col + c0
            y = jnp.dot(x, w_ref[:, lo:lo + tn], preferred_element_type=F32) + b_ref[:, lo:lo + tn]
            for l0 in range(0, tn, LANES):
                seg = y[:, l0:l0 + LANES]
                if lo < rope_cols:
                    seg = (seg * c_ref[...] + pltpu.roll(seg, LANES - ROPE_DIM // 2, 1) * sa_ref[...]
                           + pltpu.roll(seg, ROPE_DIM // 2, 1) * sb_ref[...])
                if mode is None:
                    o_ref[:, c0 + l0:c0 + l0 + LANES] = seg.astype(o_ref.dtype)
                else:
                    head = (c0 + l0) // HEAD_DIM
                    o_ref[head] = jnp.where(lower, seg, fills[mode]).astype(o_ref.dtype)
                    o_ref[head + 1] = jnp.where(lower, pltpu.roll(seg, HEAD_DIM, 1), fills[mode]).astype(o_ref.dtype)
        col += width


def _project(x2d, w_bf16, bias, rope_tabs, splits, rope_cols, seq, tn, modes=None):
    n, d = x2d.shape
    nout = w_bf16.shape[1]
    tm = min(TM_PROJ, seq)
    modes = tuple(modes) if modes is not None else (None,) * len(splits)
    assert n % tm == 0 and seq % tm == 0 and sum(splits) == nout
    assert all(s % tn == 0 for s in splits) and rope_cols % tn == 0 and tn % LANES == 0
    assert 2 * HEAD_DIM == LANES and (tm % MOBA_BLOCK == 0 or 'block' not in modes)
    pos_blocks = seq // tm
    batch = n // seq
    c_tab, sa_tab, sb_tab = rope_tabs
    tab_spec = pl.BlockSpec((tm, LANES), lambda i: (i % pos_blocks, 0))
    out_specs, out_shape = [], []
    for s, mode in zip(splits, modes):
        if mode is None:
            out_specs.append(pl.BlockSpec((tm, s), lambda i: (i, 0)))
            out_shape.append(jax.ShapeDtypeStruct((n, s), BF16))
        else:
            heads = s // HEAD_DIM
            out_specs.append(pl.BlockSpec((None, heads, tm, LANES), lambda i: (i // pos_blocks, 0, i % pos_blocks, 0)))
            out_shape.append(jax.ShapeDtypeStruct((batch, heads, seq, LANES), BF16))
    kern = functools.partial(_proj_kernel, splits=tuple(splits), modes=modes, rope_cols=rope_cols, tn=tn, tm=tm,
                             pos_blocks=pos_blocks)
    return pl.pallas_call(
        kern,
        grid=(n // tm,),
        in_specs=[
            pl.BlockSpec((tm, d), lambda i: (i, 0)),
            pl.BlockSpec((d, nout), lambda i: (0, 0)),
            pl.BlockSpec((1, nout), lambda i: (0, 0)),
            tab_spec, tab_spec, tab_spec,
        ],
        out_specs=out_specs,
        out_shape=out_shape,
        compiler_params=_cparams(("parallel",)),
        name="proj_rope",
    )(x2d, w_bf16, bias.reshape(1, nout).astype(F32), c_tab, sa_tab, sb_tab)


def _rope_tables(seq):
    half = ROPE_DIM // 2
    inv = ROPE_THETA ** (-jnp.arange(0, ROPE_DIM, 2, dtype=F32) / ROPE_DIM)
    ang = jnp.arange(seq, dtype=F32)[:, None] * inv[None, :]
    cos, sin = jnp.cos(ang), jnp.sin(ang)
    ones = jnp.ones((seq, HEAD_DIM - ROPE_DIM), F32)
    zeros = jnp.zeros((seq, HEAD_DIM - ROPE_DIM), F32)
    zh = jnp.zeros((seq, half), F32)
    c_head = jnp.concatenate([cos, cos, ones], axis=1)
    sa_head = jnp.concatenate([-sin, zh, zeros], axis=1)
    sb_head = jnp.concatenate([zh, sin, zeros], axis=1)
    rep = LANES // HEAD_DIM
    return tuple(jnp.tile(t, (1, rep)) for t in (c_head, sa_head, sb_head))


def _to_heads(t2d, batch, seq, hkv, groups):
    return t2d.reshape(batch, seq, hkv, groups, HEAD_DIM).transpose(0, 2, 3, 1, 4)


def _from_heads(t_hm):
    b, hkv, groups, s, _ = t_hm.shape
    return t_hm.transpose(0, 3, 1, 2, 4).reshape(b * s, hkv * groups * HEAD_DIM)


def _kmean_kernel(k_ref, o_ref, *, nb, blk):
    k = k_ref[...].astype(F32)
    km = jnp.mean(k.reshape(nb, blk, LANES), axis=1)
    lane = lax.broadcasted_iota(jnp.int32, (nb, LANES), 1)
    o_ref[...] = jnp.zeros(o_ref.shape, F32)
    o_ref[0:nb, :] = jnp.where(lane < HEAD_DIM, km, 0.0)


def _kmean(ka, nb):
    b, hkv, s, _ = ka.shape
    assert nb <= LANES
    return pl.pallas_call(
        functools.partial(_kmean_kernel, nb=nb, blk=MOBA_BLOCK),
        grid=(b, hkv),
        in_specs=[pl.BlockSpec((None, None, s, LANES), lambda i, j: (i, j, 0, 0))],
        out_specs=pl.BlockSpec((None, None, LANES, LANES), lambda i, j: (i, j, 0, 0)),
        out_shape=jax.ShapeDtypeStruct((b, hkv, LANES, LANES), F32),
        compiler_params=_cparams(("parallel", "parallel")),
        name="moba_kmean",
    )(ka)


def _moba_kernel(it_ref, jt_ref, q_ref, kta_ref, va_ref, km_ref, o_ref, qa_scr, m_scr, acc_scr,
                 *, groups, blk, kv_tile, nbp, scale):
    p = pl.program_id(2)
    i = it_ref[p]
    jt = jt_ref[p]
    own_tile = i // (kv_tile // blk)
    is_diag = jt == own_tile
    rows = groups * blk

    def softmax_step(s, first):
        parts = [s[:, c:c + LANES] for c in range(0, kv_tile, LANES)]
        mx = parts[0]
        for part in parts[1:]:
            mx = jnp.maximum(mx, part)
        rm = jnp.max(mx, axis=-1, keepdims=True)
        if first:
            m_new = jnp.broadcast_to(rm, (rows, LANES))
        else:
            m_old = m_scr[...]
            m_new = jnp.maximum(m_old, rm)
        pr = jnp.concatenate([jnp.exp(part - m_new) for part in parts], axis=1).astype(BF16)
        pv = jnp.dot(pr, va_ref[...], preferred_element_type=F32)
        if first:
            acc_scr[...] = pv
        else:
            acc_scr[...] = jnp.exp(m_old - m_new) * acc_scr[...] + pv
        m_scr[...] = m_new

    @pl.when(is_diag)
    def _():
        qf = q_ref[...].reshape(rows, LANES).astype(F32)
        gate = lax.dot_general(km_ref[0:nbp, :], qf, (((1,), (1,)), ((), ())),
                               precision=lax.Precision.HIGHEST, preferred_element_type=F32)
        kb = lax.broadcasted_iota(jnp.int32, (nbp, rows), 0)
        kb_f = kb.astype(F32)
        cur = jnp.where(kb < i, gate, NEG_INF)
        sel = jnp.zeros((nbp, rows), F32)
        for _ in range(MOBA_TOPK):
            mx = jnp.max(cur, axis=0, keepdims=True)
            idx = jnp.min(jnp.where(cur == mx, kb_f, float(nbp)), axis=0, keepdims=True)
            hit = kb_f == idx
            sel = jnp.where(hit & (mx > NEG_INF), 1.0, sel)
            cur = jnp.where(hit, NEG_INF, cur)
        bias_t = jnp.where((sel > 0.0) | (kb >= i), 0.0, MOBA_MASK_BIAS)
        bias_t = jnp.concatenate([jnp.zeros((HEAD_DIM, rows), F32), bias_t,
                                  jnp.zeros((LANES - HEAD_DIM - nbp, rows), F32)], axis=0)
        qa = (qf * scale + bias_t.T).astype(BF16)
        qa_scr[...] = qa
        s = jnp.dot(qa, kta_ref[...], preferred_element_type=F32)
        qpos = i * blk + lax.broadcasted_iota(jnp.int32, (groups, blk, kv_tile), 1).reshape(rows, kv_tile)
        kpos = jt * kv_tile + lax.broadcasted_iota(jnp.int32, (rows, kv_tile), 1)
        s = jnp.where(kpos <= qpos, s, NEG_INF)
        softmax_step(s, True)

    @pl.when(jnp.logical_not(is_diag))
    def _():
        s = jnp.dot(qa_scr[...], kta_ref[...], preferred_element_type=F32)
        softmax_step(s, False)

    @pl.when((jt == own_tile - 1) | (own_tile == 0))
    def _():
        acc = acc_scr[...]
        o = acc / acc[:, HEAD_DIM:HEAD_DIM + 1]
        lower = lax.broadcasted_iota(jnp.int32, (blk, LANES), 1) < HEAD_DIM
        pairs = [jnp.where(lower, o[g * blk:(g + 1) * blk], pltpu.roll(o[(g + 1) * blk:(g + 2) * blk], HEAD_DIM, 1))
                 for g in range(0, groups, 2)]
        o_ref[...] = jnp.concatenate(pairs, axis=1).astype(o_ref.dtype)


def _moba_attention(q_pad, kta, va, kmean):
    b, hkv, groups, s, _ = q_pad.shape
    blk = MOBA_BLOCK
    assert s % blk == 0 and blk == 2 * LANES
    nb = s // blk
    assert nb <= LANES - HEAD_DIM
    kv_tile = MOBA_KV_TILE if s % MOBA_KV_TILE == 0 else blk
    per_tile = kv_tile // blk
    it, jt = [], []
    for i in range(nb):
        it.append(i)
        jt.append(i // per_tile)
        for j in range(i // per_tile):
            it.append(i)
            jt.append(j)
    it = jnp.asarray(np.asarray(it, np.int32))
    jt = jnp.asarray(np.asarray(jt, np.int32))
    nbp = -(-nb // SUBLANES) * SUBLANES
    assert nbp < LANES - HEAD_DIM
    kern = functools.partial(_moba_kernel, groups=groups, blk=blk, kv_tile=kv_tile, nbp=nbp,
                             scale=HEAD_DIM ** -0.5)
    grid_spec = pltpu.PrefetchScalarGridSpec(
        num_scalar_prefetch=2,
        grid=(b, hkv, int(it.shape[0])),
        in_specs=[
            pl.BlockSpec((None, None, groups, blk, LANES), lambda bi, h, p, it, jt: (bi, h, 0, it[p], 0)),
            pl.BlockSpec((None, None, LANES, kv_tile), lambda bi, h, p, it, jt: (bi, h, 0, jt[p])),
            pl.BlockSpec((None, None, kv_tile, LANES), lambda bi, h, p, it, jt: (bi, h, jt[p], 0)),
            pl.BlockSpec((None, None, LANES, LANES), lambda bi, h, p, it, jt: (bi, h, 0, 0)),
        ],
        out_specs=pl.BlockSpec((None, blk, groups * HEAD_DIM), lambda bi, h, p, it, jt: (bi, it[p], h)),
        scratch_shapes=[
            pltpu.VMEM((groups * blk, LANES), BF16),
            pltpu.VMEM((groups * blk, LANES), F32),
            pltpu.VMEM((groups * blk, LANES), F32),
        ],
    )
    return pl.pallas_call(
        kern,
        grid_spec=grid_spec,
        out_shape=jax.ShapeDtypeStruct((b, s, hkv * groups * HEAD_DIM), BF16),
        compiler_params=_cparams(("parallel", "parallel", "arbitrary")),
        name="moba_attn",
    )(it, jt, q_pad, kta, va, kmean)


def _swa_kernel(q_ref, ktp_ref, ktc_ref, vp_ref, vc_ref, sink_ref, o_ref, *, groups, win, scale):
    kh = pl.program_id(1)
    i = pl.program_id(2)
    row = lax.broadcasted_iota(jnp.int32, (win, win), 0)
    col = lax.broadcasted_iota(jnp.int32, (win, win), 1)
    cur_mask = col <= row
    prev_mask = (col > row) & (i > 0)
    for g in range(groups):
        qg = q_ref[g]
        sp = jnp.dot(qg, ktp_ref[...], preferred_element_type=F32) * scale
        sc = jnp.dot(qg, ktc_ref[...], preferred_element_type=F32) * scale
        sp = jnp.where(prev_mask, sp, NEG_INF)
        sc = jnp.where(cur_mask, sc, NEG_INF)
        sink = sink_ref[kh * groups + g]
        mx = jnp.maximum(jnp.maximum(jnp.max(sp, axis=-1, keepdims=True), jnp.max(sc, axis=-1, keepdims=True)), sink)
        pp = jnp.exp(sp - mx)
        pc = jnp.exp(sc - mx)
        den = jnp.sum(pp, axis=-1, keepdims=True) + jnp.sum(pc, axis=-1, keepdims=True) + jnp.exp(sink - mx)
        o = (jnp.dot(pp.astype(BF16), vp_ref[...], preferred_element_type=F32)
             + jnp.dot(pc.astype(BF16), vc_ref[...], preferred_element_type=F32))
        o_ref[g] = (o / den).astype(o_ref.dtype)


def _swa_attention(q_hm, kt_hm, v_hm, sinks):
    b, hkv, groups, s, _ = q_hm.shape
    win = SWA_WINDOW
    assert s % win == 0
    nb = s // win
    kern = functools.partial(_swa_kernel, groups=groups, win=win, scale=HEAD_DIM ** -0.5)
    return pl.pallas_call(
        kern,
        grid=(b, hkv, nb),
        in_specs=[
            pl.BlockSpec((None, None, groups, win, HEAD_DIM), lambda bi, h, i: (bi, h, 0, i, 0)),
            pl.BlockSpec((None, None, HEAD_DIM, win), lambda bi, h, i: (bi, h, 0, jnp.maximum(i - 1, 0))),
            pl.BlockSpec((None, None, HEAD_DIM, win), lambda bi, h, i: (bi, h, 0, i)),
            pl.BlockSpec((None, None, win, HEAD_DIM), lambda bi, h, i: (bi, h, jnp.maximum(i - 1, 0), 0)),
            pl.BlockSpec((None, None, win, HEAD_DIM), lambda bi, h, i: (bi, h, i, 0)),
            pl.BlockSpec(memory_space=pltpu.SMEM),
        ],
        out_specs=pl.BlockSpec((None, None, groups, win, HEAD_DIM), lambda bi, h, i: (bi, h, 0, i, 0)),
        out_shape=jax.ShapeDtypeStruct(q_hm.shape, BF16),
        compiler_params=_cparams(("parallel", "parallel", "parallel")),
        name="swa_attn",
    )(q_hm, kt_hm, kt_hm, v_hm, v_hm, sinks.astype(F32))


def _layer_norm_rows(z, g, b):
    mu = jnp.mean(z, axis=-1, keepdims=True)
    zc = z - mu
    var = jnp.mean(zc * zc, axis=-1, keepdims=True)
    return zc * lax.rsqrt(var + LN_EPS) * g + b


def _oproj_kernel(a_ref, x_ref, wo_ref, bo_ref, g_ref, b_ref, wr_ref, br_ref, x1_ref, ti_ref, tg_ref,
                  *, alpha, n_experts):
    mix = jnp.dot(a_ref[...], wo_ref[...], preferred_element_type=F32) + bo_ref[...]
    x1 = _layer_norm_rows(alpha * x_ref[...] + mix, g_ref[...], b_ref[...])
    x1_ref[...] = x1
    logits = jnp.dot(x1, wr_ref[...], precision=lax.Precision.HIGHEST, preferred_element_type=F32) + br_ref[...]
    lane = lax.broadcasted_iota(jnp.int32, logits.shape, 1)
    cur = jnp.where(lane < n_experts, logits, NEG_INF)
    vals, idxs = [], []
    for _ in range(TOP_K):
        mx = jnp.max(cur, axis=-1, keepdims=True)
        idx = jnp.min(jnp.where(cur == mx, lane, LANES), axis=-1, keepdims=True)
        vals.append(mx)
        idxs.append(idx)
        cur = jnp.where(lane == idx, NEG_INF, cur)
    exps = [jnp.exp(v - vals[0]) for v in vals]
    den = exps[0]
    for e in exps[1:]:
        den = den + e
    ti = jnp.zeros(logits.shape, jnp.int32)
    tg = jnp.zeros(logits.shape, F32)
    for k in range(TOP_K):
        ti = jnp.where(lane == k, idxs[k], ti)
        tg = jnp.where(lane == k, exps[k] / den, tg)
    ti_ref[...] = ti
    tg_ref[...] = tg


def _oproj_ln_router(attn2d, x2d, wo_bf16, bo, ln_g, ln_b, w_r, b_r, alpha):
    n, d = x2d.shape
    da = attn2d.shape[1]
    tm = min(TM_OPROJ, n)
    assert n % tm == 0
    n_experts = w_r.shape[1]
    wr_pad = jnp.zeros((d, LANES), F32).at[:, :n_experts].set(w_r.astype(F32))
    br_pad = jnp.zeros((1, LANES), F32).at[0, :n_experts].set(b_r.astype(F32))
    row = lambda i: (i, 0)
    fixed = lambda i: (0, 0)
    return pl.pallas_call(
        functools.partial(_oproj_kernel, alpha=alpha, n_experts=n_experts),
        grid=(n // tm,),
        in_specs=[
            pl.BlockSpec((tm, da), row),
            pl.BlockSpec((tm, d), row),
            pl.BlockSpec((da, d), fixed),
            pl.BlockSpec((1, d), fixed),
            pl.BlockSpec((1, d), fixed),
            pl.BlockSpec((1, d), fixed),
            pl.BlockSpec((d, LANES), fixed),
            pl.BlockSpec((1, LANES), fixed),
        ],
        out_specs=[pl.BlockSpec((tm, d), row), pl.BlockSpec((tm, LANES), row), pl.BlockSpec((tm, LANES), row)],
        out_shape=[jax.ShapeDtypeStruct((n, d), F32), jax.ShapeDtypeStruct((n, LANES), jnp.int32),
                   jax.ShapeDtypeStruct((n, LANES), F32)],
        compiler_params=_cparams(("parallel",)),
        name="oproj_ln_router",
    )(attn2d, x2d, wo_bf16, bo.reshape(1, d).astype(F32), ln_g.reshape(1, d).astype(F32),
      ln_b.reshape(1, d).astype(F32), wr_pad, br_pad)


def _moe_gemm_kernel(be_ref, nu_ref, src_hbm, dst_hbm, x_hbm, wgu_ref, bgu_ref, wd_ref, bd_ref, y_hbm,
                     gtab, stab, xg0, xg1, yb0, yb1, gtsem, stsem, gsem, ssem, *, tm, hidden, hc, n_blocks):
    del be_ref
    i = pl.program_id(0)
    n_used = nu_ref[0]
    last = n_used - 1
    xg = (xg0, xg1)
    yb = (yb0, yb1)

    def src_copy(blk, s):
        return pltpu.make_async_copy(src_hbm.at[blk], gtab.at[pl.ds(s, 1)], gtsem.at[s])

    def dst_copy(blk, s):
        return pltpu.make_async_copy(dst_hbm.at[blk], stab.at[pl.ds(s, 1)], stsem.at[s])

    def gather_start(s, r):
        tok = gtab[s, r]
        pltpu.make_async_copy(x_hbm.at[pl.ds(tok, 1)], xg[s].at[pl.ds(r, 1)], gsem.at[s]).start()

    def gather_wait(s):
        pltpu.make_async_copy(x_hbm.at[pl.ds(0, tm)], xg[s], gsem.at[s]).wait()

    def scatter_start(s, r):
        row = stab[s, r]
        pltpu.make_async_copy(yb[s].at[pl.ds(r, 1)], y_hbm.at[pl.ds(row, 1)], ssem.at[s]).start()

    def scatter_wait(s):
        pltpu.make_async_copy(yb[s], y_hbm.at[pl.ds(0, tm)], ssem.at[s]).wait()

    @pl.when(i == 0)
    def _():
        src_copy(0, 0).start()
        src_copy(jnp.minimum(1, last), 1).start()
        dst_copy(n_blocks, 1).start()
        src_copy(0, 0).wait()

        def body(r, carry):
            gather_start(0, r)
            return carry
        lax.fori_loop(0, tm, body, 0, unroll=8)
        yb1[...] = jnp.zeros(yb1.shape, F32)
        tail = y_hbm.shape[0] - 2 * tm
        for half in range(2):
            pltpu.make_async_copy(yb1, y_hbm.at[pl.ds(tail + half * tm, tm)], ssem.at[1]).start()
        for half in range(2):
            pltpu.make_async_copy(yb1, y_hbm.at[pl.ds(tail + half * tm, tm)], ssem.at[1]).wait()

    def step(slot):
        nslot = 1 - slot
        src_copy(jnp.minimum(i + 2, last), slot).start()
        dst_copy(i, slot).start()
        src_copy(0, nslot).wait()
        dst_copy(0, nslot).wait()
        gather_wait(slot)

        @pl.when(i >= 1)
        def _():
            scatter_wait(slot)

        xb = xg[slot][...].astype(BF16)
        even = lax.broadcasted_iota(jnp.int32, (tm, LANES), 1) % 2 == 0
        n_chunks = hidden // hc
        rows_per_chunk = tm // n_chunks
        for ci in range(n_chunks):
            c = ci * hc
            @pl.when(n_used > 0)
            def _():
                for r in range(ci * rows_per_chunk, (ci + 1) * rows_per_chunk):
                    gather_start(nslot, r)
                    scatter_start(nslot, r)

            h = (jnp.dot(xb, wgu_ref[:, 2 * c:2 * (c + hc)], preferred_element_type=F32)
                 + bgu_ref[:, 2 * c:2 * (c + hc)])
            pieces = []
            for a in range(hc // LANES):
                acts = []
                for half in range(2):
                    hv = h[:, (2 * a + half) * LANES:(2 * a + half + 1) * LANES]
                    glu = jnp.minimum(hv, SWIGLU_LIMIT)
                    lin = jnp.clip(pltpu.roll(hv, LANES - 1, 1), -SWIGLU_LIMIT, SWIGLU_LIMIT)
                    acts.append(glu * jax.nn.sigmoid(SWIGLU_ALPHA * glu) * (lin + 1.0))
                pieces.append(jnp.where(even, acts[0], pltpu.roll(acts[1], 1, 1)).astype(BF16))
            act = jnp.concatenate(pieces, axis=1) if len(pieces) > 1 else pieces[0]
            part = jnp.dot(act, wd_ref[c:c + hc, :], preferred_element_type=F32)
            if ci == 0:
                yb[slot][...] = part + bd_ref[...]
            else:
                yb[slot][...] += part

    def drain(slot):
        nslot = 1 - slot
        src_copy(0, nslot).wait()
        dst_copy(0, nslot).wait()

        def body(r, carry):
            scatter_start(nslot, r)
            return carry
        lax.fori_loop(0, tm, body, 0, unroll=8)
        scatter_wait(slot)
        scatter_wait(nslot)
        gather_wait(slot)

    for parity in range(2):
        @pl.when((i < n_used) & (i % 2 == parity))
        def _():
            step(parity)

        @pl.when((i == n_used) & (i % 2 == parity))
        def _():
            drain(parity)


def _moe_gemm(x2d, src_tab, dst_tab, block_expert, n_used, wgu, bgu, wd, bd, tm):
    n, d = x2d.shape
    n_blocks = src_tab.shape[0]
    hidden = wd.shape[1]
    hc = min(HC_MOE, hidden)
    assert hidden % hc == 0 and hc % LANES == 0 and tm % (hidden // hc) == 0
    assert dst_tab.shape[0] == n_blocks + 1 and block_expert.shape[0] == n_blocks + 1
    w_in = lambda i, be, nu: (be[i], 0, 0)
    grid_spec = pltpu.PrefetchScalarGridSpec(
        num_scalar_prefetch=2,
        grid=(n_blocks + 1,),
        in_specs=[
            pl.BlockSpec(memory_space=pl.ANY),
            pl.BlockSpec(memory_space=pl.ANY),
            pl.BlockSpec(memory_space=pl.ANY),
            pl.BlockSpec((None, d, 2 * hidden), w_in),
            pl.BlockSpec((None, 1, 2 * hidden), w_in),
            pl.BlockSpec((None, hidden, d), w_in),
            pl.BlockSpec((None, 1, d), w_in),
        ],
        out_specs=pl.BlockSpec(memory_space=pl.ANY),
        scratch_shapes=[
            pltpu.SMEM((2, tm), jnp.int32),
            pltpu.SMEM((2, tm), jnp.int32),
            pltpu.VMEM((tm, d), F32),
            pltpu.VMEM((tm, d), F32),
            pltpu.VMEM((tm, d), F32),
            pltpu.VMEM((tm, d), F32),
            pltpu.SemaphoreType.DMA((2,)),
            pltpu.SemaphoreType.DMA((2,)),
            pltpu.SemaphoreType.DMA((2,)),
            pltpu.SemaphoreType.DMA((2,)),
        ],
    )
    y_rows = n * TOP_K + 2 * tm
    return pl.pallas_call(
        functools.partial(_moe_gemm_kernel, tm=tm, hidden=hidden, hc=hc, n_blocks=n_blocks),
        grid_spec=grid_spec,
        out_shape=jax.ShapeDtypeStruct((y_rows, d), F32),
        compiler_params=_cparams(("arbitrary",)),
        name="moe_gemm",
    )(block_expert, n_used, src_tab, dst_tab, x2d, wgu, bgu, wd, bd)


def _dispatch_tables(top_idx, tm, n_blocks):
    n = top_idx.shape[0]
    nk = n * TOP_K
    e_flat = top_idx.reshape(-1)
    onehot = (e_flat[:, None] == jnp.arange(N_EXPERTS, dtype=jnp.int32)[None, :]).astype(jnp.int32)
    csum = jnp.cumsum(onehot, axis=0)
    counts = csum[-1]
    rank = jnp.sum((csum - onehot) * onehot, axis=1)
    blocks_e = (counts + tm - 1) // tm
    bend = jnp.cumsum(blocks_e)
    bstart = bend - blocks_e
    dest = jnp.sum(onehot * bstart[None, :], axis=1) * tm + rank
    row_pair = jnp.full((n_blocks * tm,), -1, jnp.int32).at[dest].set(jnp.arange(nk, dtype=jnp.int32))
    row_pair = row_pair.reshape(n_blocks, tm)
    blk_ids = jnp.arange(n_blocks + 1, dtype=jnp.int32)
    trash = nk + (blk_ids[:n_blocks, None] % 2) * tm + jnp.arange(tm, dtype=jnp.int32)[None, :]
    src_tab = jnp.where(row_pair >= 0, row_pair // TOP_K, 0).astype(jnp.int32).reshape(n_blocks, 1, tm)
    dst_tab = jnp.where(row_pair >= 0, (row_pair % TOP_K) * n + row_pair // TOP_K, trash)
    before_first = nk + tm + jnp.arange(tm, dtype=jnp.int32)[None, :]
    dst_tab = jnp.concatenate([dst_tab, before_first], axis=0).astype(jnp.int32).reshape(n_blocks + 1, 1, tm)
    block_expert = jnp.minimum(jnp.sum((bend[None, :] <= blk_ids[:, None]).astype(jnp.int32), axis=1),
                               N_EXPERTS - 1).astype(jnp.int32)
    n_used = bend[-1:].astype(jnp.int32)
    return src_tab, dst_tab, block_expert, n_used


def _combine_kernel(*refs, alpha):
    y_refs, (x_ref, tg_ref, g_ref, b_ref, o_ref) = refs[:TOP_K], refs[TOP_K:]
    tg = tg_ref[...]
    y = tg[:, 0:1] * y_refs[0][...]
    for k in range(1, TOP_K):
        y = y + tg[:, k:k + 1] * y_refs[k][...]
    o_ref[...] = _layer_norm_rows(alpha * x_ref[...] + y, g_ref[...], b_ref[...])


def _combine_ln(y4, x1, gates, ln_g, ln_b, alpha):
    n, d = x1.shape
    tm = min(TM_COMBINE, n)
    assert n % tm == 0
    row = lambda i: (i, 0)
    fixed = lambda i: (0, 0)
    return pl.pallas_call(
        functools.partial(_combine_kernel, alpha=alpha),
        grid=(n // tm,),
        in_specs=[pl.BlockSpec((tm, d), functools.partial(lambda k, i: (k * (n // tm) + i, 0), k))
                  for k in range(TOP_K)] + [
            pl.BlockSpec((tm, d), row),
            pl.BlockSpec((tm, LANES), row),
            pl.BlockSpec((1, d), fixed),
            pl.BlockSpec((1, d), fixed),
        ],
        out_specs=pl.BlockSpec((tm, d), row),
        out_shape=jax.ShapeDtypeStruct((n, d), F32),
        compiler_params=_cparams(("parallel",)),
        name="moe_combine_ln",
    )(*([y4] * TOP_K), x1, gates, ln_g.reshape(1, d).astype(F32), ln_b.reshape(1, d).astype(F32))


def _moe_layer(x1, top_idx, gates, w_gate_up, b_gate_up, w_down, b_down, ln_g, ln_b, alpha):
    n, d = x1.shape
    tm = min(TM_MOE, n)
    n_blocks = -(-(n * TOP_K) // tm) + N_EXPERTS
    src_tab, dst_tab, block_expert, n_used = _dispatch_tables(top_idx[:, :TOP_K], tm, n_blocks)
    hidden = w_down.shape[1]
    assert hidden % LANES == 0
    wgu = w_gate_up.astype(BF16)
    bgu = b_gate_up.reshape(N_EXPERTS, 1, 2 * hidden).astype(F32)
    half = LANES // 2
    wd = (w_down.reshape(N_EXPERTS, hidden // LANES, 2, half, d).transpose(0, 1, 3, 2, 4)
          .reshape(N_EXPERTS, hidden, d).astype(BF16))
    bd = b_down.reshape(N_EXPERTS, 1, d).astype(F32)
    y4 = _moe_gemm(x1, src_tab, dst_tab, block_expert, n_used, wgu, bgu, wd, bd, tm)
    return _combine_ln(y4, x1, gates, ln_g, ln_b, alpha)


def kernel(x, w_qkv_a, w_o_a, w_q_b, b_q_b, sinks_b, w_o_b, b_o_b, w_kv_shared, b_kv_shared, ln_mix_g, ln_mix_b, ln_ffn_g, ln_ffn_b, w_router, b_router, w_gate_up, b_gate_up, w_down, b_down):
    batch, seq, d = x.shape
    depth = ln_mix_g.shape[0]
    n_a = w_qkv_a.shape[0]
    alpha = (2 * depth) ** 0.25
    attn_w = N_HEADS * HEAD_DIM
    kv_a_w = N_KV_A * HEAD_DIM
    kv_b_w = N_KV_B * HEAD_DIM
    ga, gb = N_HEADS // N_KV_A, N_HEADS // N_KV_B
    rope = _rope_tables(seq)
    xs = x.reshape(batch * seq, d).astype(F32)
    kt_sh = v_sh = None
    for layer in range(depth):
        if layer < n_a:
            q_pad, ka, va = _project(xs, w_qkv_a[layer].astype(BF16), jnp.zeros((attn_w + 2 * kv_a_w,), F32), rope,
                                     (attn_w, kv_a_w, kv_a_w), attn_w + kv_a_w, seq, tn=512,
                                     modes=('zero', 'block', 'one'))
            kmean = _kmean(ka, seq // MOBA_BLOCK)
            attn = _moba_attention(q_pad.reshape(batch, N_KV_A, ga, seq, LANES), ka.transpose(0, 1, 3, 2), va, kmean)
            attn2d = attn.reshape(batch * seq, attn_w)
            wo, bo = w_o_a[layer], jnp.zeros((d,), F32)
        else:
            jb = layer - n_a
            if jb == 0:
                k, v = _project(xs, w_kv_shared.astype(BF16), b_kv_shared, rope, (kv_b_w, kv_b_w), kv_b_w, seq,
                                tn=256)
                kt_sh = _to_heads(k, batch, seq, N_KV_B, 1)[:, :, 0].transpose(0, 1, 3, 2)
                v_sh = _to_heads(v, batch, seq, N_KV_B, 1)[:, :, 0]
            (q,) = _project(xs, w_q_b[jb].astype(BF16), b_q_b[jb], rope, (attn_w,), attn_w, seq, tn=512)
            attn2d = _from_heads(_swa_attention(_to_heads(q, batch, seq, N_KV_B, gb), kt_sh, v_sh, sinks_b[jb]))
            wo, bo = w_o_b[jb], b_o_b[jb]
        x1, top_idx, gates = _oproj_ln_router(attn2d, xs, wo.astype(BF16), bo, ln_mix_g[layer],
                                              ln_mix_b[layer], w_router[layer], b_router[layer], alpha)
        xs = _moe_layer(x1, top_idx, gates, w_gate_up[layer], b_gate_up[layer], w_down[layer], b_down[layer],
                        ln_ffn_g[layer], ln_ffn_b[layer], alpha)
    return xs.reshape(batch, seq, d).astype(x.dtype)
```

```python
import functools

import numpy as np
import jax
import jax.numpy as jnp
from jax import lax
from jax.experimental import pallas as pl
from jax.experimental.pallas import tpu as pltpu

F32 = jnp.float32
BF16 = jnp.bfloat16
NEG_INF = float("-inf")

HEAD_DIM = 64
N_HEADS = 32
N_KV_A = 8
N_KV_B = 4
ROPE_DIM = 16
ROPE_THETA = 500000.0
MOBA_BLOCK = 256
MOBA_TOPK = 3
SWA_WINDOW = 128
N_EXPERTS = 32
TOP_K = 4
SWIGLU_LIMIT = 7.0
SWIGLU_ALPHA = 1.702
LN_EPS = 1e-5

MOBA_MASK_BIAS = -(2.0 ** 100)

LANES = 128
SUBLANES = 8
V7X_VMEM_BYTES = 64 * 1024 * 1024
VMEM_LIMIT = V7X_VMEM_BYTES - 8 * 1024 * 1024

MOBA_KV_TILE = 512
TM_PROJ = 512
TM_OPROJ = 256
TM_MOE = 512
TM_COMBINE = 256
HC_MOE = 256
TR_SPLIT = 1024


def _cparams(semantics):
    return pltpu.CompilerParams(dimension_semantics=semantics, vmem_limit_bytes=VMEM_LIMIT)


def _proj_kernel(x_ref, w_ref, b_ref, c_ref, sa_ref, sb_ref, *o_refs, splits, modes, rope_cols, tn, tm, pos_blocks):
    x = x_ref[...].astype(BF16)
    lane = lax.broadcasted_iota(jnp.int32, (tm, LANES), 1)
    lower = lane < HEAD_DIM
    fills = {}
    if 'zero' in modes:
        fills['zero'] = jnp.zeros((tm, LANES), F32)
    if 'one' in modes:
        fills['one'] = jnp.where(lane == HEAD_DIM, 1.0, 0.0)
    if 'block' in modes:
        pos = (pl.program_id(0) % pos_blocks) * tm + lax.broadcasted_iota(jnp.int32, (tm, LANES), 0)
        fills['block'] = jnp.where(lane - HEAD_DIM == pos // MOBA_BLOCK, 1.0, 0.0)
    col = 0
    for o_ref, width, mode in zip(o_refs, splits, modes):
        for c0 in range(0, width, tn):
            lo = col + c0
            y = jnp.dot(x, w_ref[:, lo:lo + tn], preferred_element_type=F32) + b_ref[:, lo:lo + tn]
            for l0 in range(0, tn, LANES):
                seg = y[:, l0:l0 + LANES]
                if lo < rope_cols:
                    seg = (seg * c_ref[...] + pltpu.roll(seg, LANES - ROPE_DIM // 2, 1) * sa_ref[...]
                           + pltpu.roll(seg, ROPE_DIM // 2, 1) * sb_ref[...])
                if mode is None:
                    o_ref[:, c0 + l0:c0 + l0 + LANES] = seg.astype(o_ref.dtype)
                else:
                    head = (c0 + l0) // HEAD_DIM
                    o_ref[head] = jnp.where(lower, seg, fills[mode]).astype(o_ref.dtype)
                    o_ref[head + 1] = jnp.where(lower, pltpu.roll(seg, HEAD_DIM, 1), fills[mode]).astype(o_ref.dtype)
        col += width


def _project(x2d, w_bf16, bias, rope_tabs, splits, rope_cols, seq, tn, modes=None):
    n, d = x2d.shape
    nout = w_bf16.shape[1]
    tm = min(TM_PROJ, seq)
    modes = tuple(modes) if modes is not None else (None,) * len(splits)
    assert n % tm == 0 and seq % tm == 0 and sum(splits) == nout
    assert all(s % tn == 0 for s in splits) and rope_cols % tn == 0 and tn % LANES == 0
    assert 2 * HEAD_DIM == LANES and (tm % MOBA_BLOCK == 0 or 'block' not in modes)
    pos_blocks = seq // tm
    batch = n // seq
    c_tab, sa_tab, sb_tab = rope_tabs
    tab_spec = pl.BlockSpec((tm, LANES), lambda i: (i % pos_blocks, 0))
    out_specs, out_shape = [], []
    for s, mode in zip(splits, modes):
        if mode is None:
            out_specs.append(pl.BlockSpec((tm, s), lambda i: (i, 0)))
            out_shape.append(jax.ShapeDtypeStruct((n, s), BF16))
        else:
            heads = s // HEAD_DIM
            out_specs.append(pl.BlockSpec((None, heads, tm, LANES), lambda i: (i // pos_blocks, 0, i % pos_blocks, 0)))
            out_shape.append(jax.ShapeDtypeStruct((batch, heads, seq, LANES), BF16))
    kern = functools.partial(_proj_kernel, splits=tuple(splits), modes=modes, rope_cols=rope_cols, tn=tn, tm=tm,
                             pos_blocks=pos_blocks)
    return pl.pallas_call(
        kern,
        grid=(n // tm,),
        in_specs=[
            pl.BlockSpec((tm, d), lambda i: (i, 0)),
            pl.BlockSpec((d, nout), lambda i: (0, 0)),
            pl.BlockSpec((1, nout), lambda i: (0, 0)),
            tab_spec, tab_spec, tab_spec,
        ],
        out_specs=out_specs,
        out_shape=out_shape,
        compiler_params=_cparams(("parallel",)),
        name="proj_rope",
    )(x2d, w_bf16, bias.reshape(1, nout).astype(F32), c_tab, sa_tab, sb_tab)


def _rope_tables(seq):
    half = ROPE_DIM // 2
    inv = ROPE_THETA ** (-jnp.arange(0, ROPE_DIM, 2, dtype=F32) / ROPE_DIM)
    ang = jnp.arange(seq, dtype=F32)[:, None] * inv[None, :]
    cos, sin = jnp.cos(ang), jnp.sin(ang)
    ones = jnp.ones((seq, HEAD_DIM - ROPE_DIM), F32)
    zeros = jnp.zeros((seq, HEAD_DIM - ROPE_DIM), F32)
    zh = jnp.zeros((seq, half), F32)
    c_head = jnp.concatenate([cos, cos, ones], axis=1)
    sa_head = jnp.concatenate([-sin, zh, zeros], axis=1)
    sb_head = jnp.concatenate([zh, sin, zeros], axis=1)
    rep = LANES // HEAD_DIM
    return tuple(jnp.tile(t, (1, rep)) for t in (c_head, sa_head, sb_head))


def _kmean_kernel(k_ref, o_ref, *, nb, blk):
    k = k_ref[...].astype(F32)
    km = jnp.mean(k.reshape(nb, blk, LANES), axis=1)
    lane = lax.broadcasted_iota(jnp.int32, (nb, LANES), 1)
    o_ref[...] = jnp.zeros(o_ref.shape, F32)
    o_ref[0:nb, :] = jnp.where(lane < HEAD_DIM, km, 0.0)


def _kmean(ka, nb):
    b, hkv, s, _ = ka.shape
    assert nb <= LANES
    return pl.pallas_call(
        functools.partial(_kmean_kernel, nb=nb, blk=MOBA_BLOCK),
        grid=(b, hkv),
        in_specs=[pl.BlockSpec((None, None, s, LANES), lambda i, j: (i, j, 0, 0))],
        out_specs=pl.BlockSpec((None, None, LANES, LANES), lambda i, j: (i, j, 0, 0)),
        out_shape=jax.ShapeDtypeStruct((b, hkv, LANES, LANES), F32),
        compiler_params=_cparams(("parallel", "parallel")),
        name="moba_kmean",
    )(ka)


def _moba_kernel(it_ref, jt_ref, q_ref, kta_ref, va_ref, km_ref, o_ref, qa_scr, m_scr, acc_scr,
                 *, groups, blk, kv_tile, nbp, scale):
    p = pl.program_id(2)
    i = it_ref[p]
    jt = jt_ref[p]
    own_tile = i // (kv_tile // blk)
    is_diag = jt == own_tile
    rows = groups * blk

    def softmax_step(s, first):
        parts = [s[:, c:c + LANES] for c in range(0, kv_tile, LANES)]
        mx = parts[0]
        for part in parts[1:]:
            mx = jnp.maximum(mx, part)
        rm = jnp.max(mx, axis=-1, keepdims=True)
        if first:
            m_new = jnp.broadcast_to(rm, (rows, LANES))
        else:
            m_old = m_scr[...]
            m_new = jnp.maximum(m_old, rm)
        pr = jnp.concatenate([jnp.exp(part - m_new) for part in parts], axis=1).astype(BF16)
        pv = jnp.dot(pr, va_ref[...], preferred_element_type=F32)
        if first:
            acc_scr[...] = pv
        else:
            acc_scr[...] = jnp.exp(m_old - m_new) * acc_scr[...] + pv
        m_scr[...] = m_new

    @pl.when(is_diag)
    def _():
        qf = q_ref[...].reshape(rows, LANES).astype(F32)
        gate = lax.dot_general(km_ref[0:nbp, :], qf, (((1,), (1,)), ((), ())),
                               precision=lax.Precision.HIGHEST, preferred_element_type=F32)
        kb = lax.broadcasted_iota(jnp.int32, (nbp, rows), 0)
        kb_f = kb.astype(F32)
        cur = jnp.where(kb < i, gate, NEG_INF)
        sel = jnp.zeros((nbp, rows), F32)
        for _ in range(MOBA_TOPK):
            mx = jnp.max(cur, axis=0, keepdims=True)
            idx = jnp.min(jnp.where(cur == mx, kb_f, float(nbp)), axis=0, keepdims=True)
            hit = kb_f == idx
            sel = jnp.where(hit & (mx > NEG_INF), 1.0, sel)
            cur = jnp.where(hit, NEG_INF, cur)
        bias_t = jnp.where((sel > 0.0) | (kb >= i), 0.0, MOBA_MASK_BIAS)
        bias_t = jnp.concatenate([jnp.zeros((HEAD_DIM, rows), F32), bias_t,
                                  jnp.zeros((LANES - HEAD_DIM - nbp, rows), F32)], axis=0)
        qa = (qf * scale + bias_t.T).astype(BF16)
        qa_scr[...] = qa
        s = jnp.dot(qa, kta_ref[...], preferred_element_type=F32)
        qpos = i * blk + lax.broadcasted_iota(jnp.int32, (groups, blk, kv_tile), 1).reshape(rows, kv_tile)
        kpos = jt * kv_tile + lax.broadcasted_iota(jnp.int32, (rows, kv_tile), 1)
        s = jnp.where(kpos <= qpos, s, NEG_INF)
        softmax_step(s, True)

    @pl.when(jnp.logical_not(is_diag))
    def _():
        s = jnp.dot(qa_scr[...], kta_ref[...], preferred_element_type=F32)
        softmax_step(s, False)

    @pl.when((jt == own_tile - 1) | (own_tile == 0))
    def _():
        acc = acc_scr[...]
        o = acc / acc[:, HEAD_DIM:HEAD_DIM + 1]
        lower = lax.broadcasted_iota(jnp.int32, (blk, LANES), 1) < HEAD_DIM
        pairs = [jnp.where(lower, o[g * blk:(g + 1) * blk], pltpu.roll(o[(g + 1) * blk:(g + 2) * blk], HEAD_DIM, 1))
                 for g in range(0, groups, 2)]
        o_ref[...] = jnp.concatenate(pairs, axis=1).astype(o_ref.dtype)


def _moba_attention(q_pad, kta, va, kmean):
    b, hkv, groups, s, _ = q_pad.shape
    blk = MOBA_BLOCK
    assert s % blk == 0 and blk == 2 * LANES
    nb = s // blk
    assert nb <= LANES - HEAD_DIM
    kv_tile = MOBA_KV_TILE if s % MOBA_KV_TILE == 0 else blk
    per_tile = kv_tile // blk
    it, jt = [], []
    for i in range(nb):
        it.append(i)
        jt.append(i // per_tile)
        for j in range(i // per_tile):
            it.append(i)
            jt.append(j)
    it = jnp.asarray(np.asarray(it, np.int32))
    jt = jnp.asarray(np.asarray(jt, np.int32))
    nbp = -(-nb // SUBLANES) * SUBLANES
    assert nbp < LANES - HEAD_DIM
    kern = functools.partial(_moba_kernel, groups=groups, blk=blk, kv_tile=kv_tile, nbp=nbp,
                             scale=HEAD_DIM ** -0.5)
    grid_spec = pltpu.PrefetchScalarGridSpec(
        num_scalar_prefetch=2,
        grid=(b, hkv, int(it.shape[0])),
        in_specs=[
            pl.BlockSpec((None, None, groups, blk, LANES), lambda bi, h, p, it, jt: (bi, h, 0, it[p], 0)),
            pl.BlockSpec((None, None, LANES, kv_tile), lambda bi, h, p, it, jt: (bi, h, 0, jt[p])),
            pl.BlockSpec((None, None, kv_tile, LANES), lambda bi, h, p, it, jt: (bi, h, jt[p], 0)),
            pl.BlockSpec((None, None, LANES, LANES), lambda bi, h, p, it, jt: (bi, h, 0, 0)),
        ],
        out_specs=pl.BlockSpec((None, blk, groups * HEAD_DIM), lambda bi, h, p, it, jt: (bi, it[p], h)),
        scratch_shapes=[
            pltpu.VMEM((groups * blk, LANES), BF16),
            pltpu.VMEM((groups * blk, LANES), F32),
            pltpu.VMEM((groups * blk, LANES), F32),
        ],
    )
    return pl.pallas_call(
        kern,
        grid_spec=grid_spec,
        out_shape=jax.ShapeDtypeStruct((b, s, hkv * groups * HEAD_DIM), BF16),
        compiler_params=_cparams(("parallel", "parallel", "arbitrary")),
        name="moba_attn",
    )(it, jt, q_pad, kta, va, kmean)


def _swa_kernel(q_ref, ktp_ref, ktc_ref, vp_ref, vc_ref, sink_ref, o_ref, *, groups, win, scale):
    kh = pl.program_id(1)
    i = pl.program_id(2)
    row = lax.broadcasted_iota(jnp.int32, (win, win), 0)
    col = lax.broadcasted_iota(jnp.int32, (win, win), 1)
    cur_mask = col <= row
    prev_mask = (col > row) & (i > 0)
    outs = []
    for g in range(groups):
        qg = q_ref[g]
        sp = jnp.dot(qg, ktp_ref[...], preferred_element_type=F32) * scale
        sc = jnp.dot(qg, ktc_ref[...], preferred_element_type=F32) * scale
        sp = jnp.where(prev_mask, sp, NEG_INF)
        sc = jnp.where(cur_mask, sc, NEG_INF)
        sink = sink_ref[kh * groups + g]
        mx = jnp.maximum(jnp.maximum(jnp.max(sp, axis=-1, keepdims=True), jnp.max(sc, axis=-1, keepdims=True)), sink)
        pp = jnp.exp(sp - mx)
        pc = jnp.exp(sc - mx)
        den = jnp.sum(pp, axis=-1, keepdims=True) + jnp.sum(pc, axis=-1, keepdims=True) + jnp.exp(sink - mx)
        o = (jnp.dot(pp.astype(BF16), vp_ref[...], preferred_element_type=F32)
             + jnp.dot(pc.astype(BF16), vc_ref[...], preferred_element_type=F32))
        outs.append(o / den)
    lower = lax.broadcasted_iota(jnp.int32, (win, LANES), 1) < HEAD_DIM
    pairs = [jnp.where(lower, outs[g], pltpu.roll(outs[g + 1], HEAD_DIM, 1)) for g in range(0, groups, 2)]
    o_ref[...] = jnp.concatenate(pairs, axis=1).astype(o_ref.dtype)


def _swa_attention(q_pad, kt_pad, v_pad, sinks):
    b, hkv, groups, s, _ = q_pad.shape
    win = SWA_WINDOW
    assert s % win == 0 and groups % 2 == 0
    nb = s // win
    kern = functools.partial(_swa_kernel, groups=groups, win=win, scale=HEAD_DIM ** -0.5)
    return pl.pallas_call(
        kern,
        grid=(b, hkv, nb),
        in_specs=[
            pl.BlockSpec((None, None, groups, win, LANES), lambda bi, h, i: (bi, h, 0, i, 0)),
            pl.BlockSpec((None, None, LANES, win), lambda bi, h, i: (bi, h, 0, jnp.maximum(i - 1, 0))),
            pl.BlockSpec((None, None, LANES, win), lambda bi, h, i: (bi, h, 0, i)),
            pl.BlockSpec((None, None, win, LANES), lambda bi, h, i: (bi, h, jnp.maximum(i - 1, 0), 0)),
            pl.BlockSpec((None, None, win, LANES), lambda bi, h, i: (bi, h, i, 0)),
            pl.BlockSpec(memory_space=pltpu.SMEM),
        ],
        out_specs=pl.BlockSpec((None, win, groups * HEAD_DIM), lambda bi, h, i: (bi, i, h)),
        out_shape=jax.ShapeDtypeStruct((b, s, hkv * groups * HEAD_DIM), BF16),
        compiler_params=_cparams(("parallel", "parallel", "parallel")),
        name="swa_attn",
    )(q_pad, kt_pad, kt_pad, v_pad, v_pad, sinks.astype(F32))


def _layer_norm_rows(z, g, b):
    mu = jnp.mean(z, axis=-1, keepdims=True)
    zc = z - mu
    var = jnp.mean(zc * zc, axis=-1, keepdims=True)
    return zc * lax.rsqrt(var + LN_EPS) * g + b


def _oproj_kernel(a_ref, x_ref, wo_ref, bo_ref, g_ref, b_ref, wr_ref, br_ref, x1_ref, ti_ref, tg_ref,
                  *, alpha, n_experts):
    mix = jnp.dot(a_ref[...], wo_ref[...], preferred_element_type=F32) + bo_ref[...]
    x1 = _layer_norm_rows(alpha * x_ref[...] + mix, g_ref[...], b_ref[...])
    x1_ref[...] = x1
    logits = jnp.dot(x1, wr_ref[...], precision=lax.Precision.HIGHEST, preferred_element_type=F32) + br_ref[...]
    lane = lax.broadcasted_iota(jnp.int32, logits.shape, 1)
    cur = jnp.where(lane < n_experts, logits, NEG_INF)
    vals, idxs = [], []
    for _ in range(TOP_K):
        mx = jnp.max(cur, axis=-1, keepdims=True)
        idx = jnp.min(jnp.where(cur == mx, lane, LANES), axis=-1, keepdims=True)
        vals.append(mx)
        idxs.append(idx)
        cur = jnp.where(lane == idx, NEG_INF, cur)
    exps = [jnp.exp(v - vals[0]) for v in vals]
    den = exps[0]
    for e in exps[1:]:
        den = den + e
    ti = jnp.zeros(logits.shape, jnp.int32)
    tg = jnp.zeros(logits.shape, F32)
    for k in range(TOP_K):
        ti = jnp.where(lane == k, idxs[k], ti)
        tg = jnp.where(lane == k, exps[k] / den, tg)
    ti_ref[...] = ti
    tg_ref[...] = tg


def _oproj_ln_router(attn2d, x2d, wo_bf16, bo, ln_g, ln_b, w_r, b_r, alpha):
    n, d = x2d.shape
    da = attn2d.shape[1]
    tm = min(TM_OPROJ, n)
    assert n % tm == 0
    n_experts = w_r.shape[1]
    wr_pad = jnp.zeros((d, LANES), F32).at[:, :n_experts].set(w_r.astype(F32))
    br_pad = jnp.zeros((1, LANES), F32).at[0, :n_experts].set(b_r.astype(F32))
    row = lambda i: (i, 0)
    fixed = lambda i: (0, 0)
    return pl.pallas_call(
        functools.partial(_oproj_kernel, alpha=alpha, n_experts=n_experts),
        grid=(n // tm,),
        in_specs=[
            pl.BlockSpec((tm, da), row),
            pl.BlockSpec((tm, d), row),
            pl.BlockSpec((da, d), fixed),
            pl.BlockSpec((1, d), fixed),
            pl.BlockSpec((1, d), fixed),
            pl.BlockSpec((1, d), fixed),
            pl.BlockSpec((d, LANES), fixed),
            pl.BlockSpec((1, LANES), fixed),
        ],
        out_specs=[pl.BlockSpec((tm, d), row), pl.BlockSpec((tm, LANES), row), pl.BlockSpec((tm, LANES), row)],
        out_shape=[jax.ShapeDtypeStruct((n, d), F32), jax.ShapeDtypeStruct((n, LANES), jnp.int32),
                   jax.ShapeDtypeStruct((n, LANES), F32)],
        compiler_params=_cparams(("parallel",)),
        name="oproj_ln_router",
    )(attn2d, x2d, wo_bf16, bo.reshape(1, d).astype(F32), ln_g.reshape(1, d).astype(F32),
      ln_b.reshape(1, d).astype(F32), wr_pad, br_pad)


def _split_gate_up_kernel(w_ref, pg_ref, pu_ref, g_ref, u_ref, *, width):
    for c in range(0, width, 2 * LANES):
        blk = w_ref[:, c:c + 2 * LANES].astype(BF16)
        g_ref[:, c // 2:c // 2 + LANES] = jnp.dot(blk, pg_ref[...], preferred_element_type=F32).astype(BF16)
        u_ref[:, c // 2:c // 2 + LANES] = jnp.dot(blk, pu_ref[...], preferred_element_type=F32).astype(BF16)


def _split_gate_up(w_gate_up):
    nl, ne, d, width = w_gate_up.shape
    assert width % (2 * LANES) == 0
    tr = min(TR_SPLIT, d)
    assert d % tr == 0
    k = jnp.arange(2 * LANES, dtype=jnp.int32)[:, None]
    j = jnp.arange(LANES, dtype=jnp.int32)[None, :]
    pick_gate = (k == 2 * j).astype(BF16)
    pick_lin = (k == 2 * j + 1).astype(BF16)
    out = jax.ShapeDtypeStruct((nl * ne, d, width // 2), BF16)
    return pl.pallas_call(
        functools.partial(_split_gate_up_kernel, width=width),
        grid=(nl * ne, d // tr),
        in_specs=[
            pl.BlockSpec((None, tr, width), lambda e, r: (e, r, 0)),
            pl.BlockSpec((2 * LANES, LANES), lambda e, r: (0, 0)),
            pl.BlockSpec((2 * LANES, LANES), lambda e, r: (0, 0)),
        ],
        out_specs=[pl.BlockSpec((None, tr, width // 2), lambda e, r: (e, r, 0))] * 2,
        out_shape=[out, out],
        compiler_params=_cparams(("parallel", "parallel")),
        name="split_gate_up",
    )(w_gate_up.reshape(nl * ne, d, width), pick_gate, pick_lin)


def _moe_gemm_kernel(be_ref, nu_ref, src_hbm, dst_hbm, x_hbm, wg_ref, wu_ref, bg_ref, bu_ref, wd_ref, bd_ref, y_hbm,
                     gtab, stab, xg0, xg1, yb0, yb1, gtsem, stsem, gsem, ssem, *, tm, hidden, hc, n_blocks):
    del be_ref
    i = pl.program_id(0)
    n_used = nu_ref[0]
    last = n_used - 1
    xg = (xg0, xg1)
    yb = (yb0, yb1)

    def src_copy(blk, s):
        return pltpu.make_async_copy(src_hbm.at[blk], gtab.at[pl.ds(s, 1)], gtsem.at[s])

    def dst_copy(blk, s):
        return pltpu.make_async_copy(dst_hbm.at[blk], stab.at[pl.ds(s, 1)], stsem.at[s])

    def gather_start(s, r, priority=0):
        tok = gtab[s, r]
        pltpu.make_async_copy(x_hbm.at[pl.ds(tok, 1)], xg[s].at[pl.ds(r, 1)], gsem.at[s]).start(priority)

    def gather_wait(s):
        pltpu.make_async_copy(x_hbm.at[pl.ds(0, tm)], xg[s], gsem.at[s]).wait()

    def scatter_start(s, r, priority=0):
        row = stab[s, r]
        pltpu.make_async_copy(yb[s].at[pl.ds(r, 1)], y_hbm.at[pl.ds(row, 1)], ssem.at[s]).start(priority)

    def scatter_wait(s):
        pltpu.make_async_copy(yb[s], y_hbm.at[pl.ds(0, tm)], ssem.at[s]).wait()

    @pl.when(i == 0)
    def _():
        src_copy(0, 0).start()
        src_copy(jnp.minimum(1, last), 1).start()
        dst_copy(n_blocks, 1).start()
        src_copy(0, 0).wait()

        def body(r, carry):
            gather_start(0, r)
            return carry
        lax.fori_loop(0, tm, body, 0, unroll=8)
        yb1[...] = jnp.zeros(yb1.shape, F32)
        tail = y_hbm.shape[0] - 2 * tm
        for half in range(2):
            pltpu.make_async_copy(yb1, y_hbm.at[pl.ds(tail + half * tm, tm)], ssem.at[1]).start()
        for half in range(2):
            pltpu.make_async_copy(yb1, y_hbm.at[pl.ds(tail + half * tm, tm)], ssem.at[1]).wait()

    def step(slot):
        nslot = 1 - slot
        src_copy(jnp.minimum(i + 2, last), slot).start()
        dst_copy(i, slot).start()
        src_copy(0, nslot).wait()
        dst_copy(0, nslot).wait()
        gather_wait(slot)

        @pl.when(i >= 1)
        def _():
            scatter_wait(slot)

        xb = xg[slot][...].astype(BF16)
        n_chunks = hidden // hc
        rows_per_chunk = tm // n_chunks
        for ci in range(n_chunks):
            c = ci * hc
            @pl.when(n_used > 0)
            def _():
                for r in range(ci * rows_per_chunk, (ci + 1) * rows_per_chunk):
                    gather_start(nslot, r, priority=r % 2)
                    scatter_start(nslot, r, priority=r % 2)

            glu = jnp.dot(xb, wg_ref[:, c:c + hc], preferred_element_type=F32) + bg_ref[:, c:c + hc]
            lin = jnp.dot(xb, wu_ref[:, c:c + hc], preferred_element_type=F32) + bu_ref[:, c:c + hc]
            glu = jnp.minimum(glu, SWIGLU_LIMIT)
            lin = jnp.clip(lin, -SWIGLU_LIMIT, SWIGLU_LIMIT)
            act = glu * jax.nn.sigmoid(SWIGLU_ALPHA * glu) * (lin + 1.0)
            part = jnp.dot(act.astype(BF16), wd_ref[c:c + hc, :], preferred_element_type=F32)
            if ci == 0:
                yb[slot][...] = part + bd_ref[...]
            else:
                yb[slot][...] += part

    def drain(slot):
        nslot = 1 - slot
        src_copy(0, nslot).wait()
        dst_copy(0, nslot).wait()

        def body(r, carry):
            scatter_start(nslot, r)
            return carry
        lax.fori_loop(0, tm, body, 0, unroll=8)
        scatter_wait(slot)
        scatter_wait(nslot)
        gather_wait(slot)

    for parity in range(2):
        @pl.when((i < n_used) & (i % 2 == parity))
        def _():
            step(parity)

        @pl.when((i == n_used) & (i % 2 == parity))
        def _():
            drain(parity)


def _moe_gemm(x2d, src_tab, dst_tab, block_expert, n_used, layer, wg, wu, bg, bu, wd, bd, tm):
    n, d = x2d.shape
    n_blocks = src_tab.shape[0]
    n_experts, hidden = wd.shape[1], wd.shape[2]
    hc = min(HC_MOE, hidden)
    assert hidden % hc == 0 and hc % LANES == 0 and tm % (hidden // hc) == 0
    assert dst_tab.shape[0] == n_blocks + 1 and block_expert.shape[0] == n_blocks + 1
    w_flat = lambda i, be, nu: (layer * n_experts + be[i], 0, 0)
    w_in = lambda i, be, nu: (be[i], 0, 0)
    grid_spec = pltpu.PrefetchScalarGridSpec(
        num_scalar_prefetch=2,
        grid=(n_blocks + 1,),
        in_specs=[
            pl.BlockSpec(memory_space=pl.ANY),
            pl.BlockSpec(memory_space=pl.ANY),
            pl.BlockSpec(memory_space=pl.ANY),
            pl.BlockSpec((None, d, hidden), w_flat),
            pl.BlockSpec((None, d, hidden), w_flat),
            pl.BlockSpec((None, 1, hidden), w_in),
            pl.BlockSpec((None, 1, hidden), w_in),
            pl.BlockSpec((None, None, hidden, d), lambda i, be, nu: (layer, be[i], 0, 0)),
            pl.BlockSpec((None, 1, d), w_in),
        ],
        out_specs=pl.BlockSpec(memory_space=pl.ANY),
        scratch_shapes=[
            pltpu.SMEM((2, tm), jnp.int32),
            pltpu.SMEM((2, tm), jnp.int32),
            pltpu.VMEM((tm, d), F32),
            pltpu.VMEM((tm, d), F32),
            pltpu.VMEM((tm, d), F32),
            pltpu.VMEM((tm, d), F32),
            pltpu.SemaphoreType.DMA((2,)),
            pltpu.SemaphoreType.DMA((2,)),
            pltpu.SemaphoreType.DMA((2,)),
            pltpu.SemaphoreType.DMA((2,)),
        ],
    )
    y_rows = n * TOP_K + 2 * tm
    return pl.pallas_call(
        functools.partial(_moe_gemm_kernel, tm=tm, hidden=hidden, hc=hc, n_blocks=n_blocks),
        grid_spec=grid_spec,
        out_shape=jax.ShapeDtypeStruct((y_rows, d), F32),
        compiler_params=_cparams(("arbitrary",)),
        name="moe_gemm",
    )(block_expert, n_used, src_tab, dst_tab, x2d, wg, wu, bg, bu, wd, bd)


def _dispatch_tables(top_idx, tm, n_blocks):
    n = top_idx.shape[0]
    nk = n * TOP_K
    e_flat = top_idx.reshape(-1)
    onehot = (e_flat[:, None] == jnp.arange(N_EXPERTS, dtype=jnp.int32)[None, :]).astype(jnp.int32)
    csum = jnp.cumsum(onehot, axis=0)
    counts = csum[-1]
    rank = jnp.sum((csum - onehot) * onehot, axis=1)
    blocks_e = (counts + tm - 1) // tm
    bend = jnp.cumsum(blocks_e)
    bstart = bend - blocks_e
    dest = jnp.sum(onehot * bstart[None, :], axis=1) * tm + rank
    row_pair = jnp.full((n_blocks * tm,), -1, jnp.int32).at[dest].set(jnp.arange(nk, dtype=jnp.int32))
    row_pair = row_pair.reshape(n_blocks, tm)
    blk_ids = jnp.arange(n_blocks + 1, dtype=jnp.int32)
    trash = nk + (blk_ids[:n_blocks, None] % 2) * tm + jnp.arange(tm, dtype=jnp.int32)[None, :]
    src_tab = jnp.where(row_pair >= 0, row_pair // TOP_K, 0).astype(jnp.int32).reshape(n_blocks, 1, tm)
    dst_tab = jnp.where(row_pair >= 0, (row_pair % TOP_K) * n + row_pair // TOP_K, trash)
    before_first = nk + tm + jnp.arange(tm, dtype=jnp.int32)[None, :]
    dst_tab = jnp.concatenate([dst_tab, before_first], axis=0).astype(jnp.int32).reshape(n_blocks + 1, 1, tm)
    block_expert = jnp.minimum(jnp.sum((bend[None, :] <= blk_ids[:, None]).astype(jnp.int32), axis=1),
                               N_EXPERTS - 1).astype(jnp.int32)
    n_used = bend[-1:].astype(jnp.int32)
    return src_tab, dst_tab, block_expert, n_used


def _combine_kernel(*refs, alpha):
    y_refs, (x_ref, tg_ref, g_ref, b_ref, o_ref) = refs[:TOP_K], refs[TOP_K:]
    tg = tg_ref[...]
    y = tg[:, 0:1] * y_refs[0][...]
    for k in range(1, TOP_K):
        y = y + tg[:, k:k + 1] * y_refs[k][...]
    o_ref[...] = _layer_norm_rows(alpha * x_ref[...] + y, g_ref[...], b_ref[...])


def _combine_ln(y4, x1, gates, ln_g, ln_b, alpha):
    n, d = x1.shape
    tm = min(TM_COMBINE, n)
    assert n % tm == 0
    row = lambda i: (i, 0)
    fixed = lambda i: (0, 0)
    return pl.pallas_call(
        functools.partial(_combine_kernel, alpha=alpha),
        grid=(n // tm,),
        in_specs=[pl.BlockSpec((tm, d), functools.partial(lambda k, i: (k * (n // tm) + i, 0), k))
                  for k in range(TOP_K)] + [
            pl.BlockSpec((tm, d), row),
            pl.BlockSpec((tm, LANES), row),
            pl.BlockSpec((1, d), fixed),
            pl.BlockSpec((1, d), fixed),
        ],
        out_specs=pl.BlockSpec((tm, d), row),
        out_shape=jax.ShapeDtypeStruct((n, d), F32),
        compiler_params=_cparams(("parallel",)),
        name="moe_combine_ln",
    )(*([y4] * TOP_K), x1, gates, ln_g.reshape(1, d).astype(F32), ln_b.reshape(1, d).astype(F32))


def _moe_layer(x1, top_idx, gates, layer, wg, wu, b_gate_up, wd, b_down, ln_g, ln_b, alpha):
    n, d = x1.shape
    tm = min(TM_MOE, n)
    n_blocks = -(-(n * TOP_K) // tm) + N_EXPERTS
    src_tab, dst_tab, block_expert, n_used = _dispatch_tables(top_idx[:, :TOP_K], tm, n_blocks)
    hidden = wd.shape[2]
    bg = b_gate_up[:, 0::2].reshape(N_EXPERTS, 1, hidden).astype(F32)
    bu = b_gate_up[:, 1::2].reshape(N_EXPERTS, 1, hidden).astype(F32)
    bd = b_down.reshape(N_EXPERTS, 1, d).astype(F32)
    y4 = _moe_gemm(x1, src_tab, dst_tab, block_expert, n_used, layer, wg, wu, bg, bu, wd, bd, tm)
    return _combine_ln(y4, x1, gates, ln_g, ln_b, alpha)


def kernel(x, w_qkv_a, w_o_a, w_q_b, b_q_b, sinks_b, w_o_b, b_o_b, w_kv_shared, b_kv_shared, ln_mix_g, ln_mix_b, ln_ffn_g, ln_ffn_b, w_router, b_router, w_gate_up, b_gate_up, w_down, b_down):
    batch, seq, d = x.shape
    depth = ln_mix_g.shape[0]
    n_a = w_qkv_a.shape[0]
    alpha = (2 * depth) ** 0.25
    attn_w = N_HEADS * HEAD_DIM
    kv_a_w = N_KV_A * HEAD_DIM
    kv_b_w = N_KV_B * HEAD_DIM
    ga, gb = N_HEADS // N_KV_A, N_HEADS // N_KV_B
    rope = _rope_tables(seq)
    xs = x.reshape(batch * seq, d).astype(F32)
    kt_sh = v_sh = None
    wg_all, wu_all = _split_gate_up(w_gate_up)
    wd_all = w_down.astype(BF16)
    for layer in range(depth):
        if layer < n_a:
            q_pad, ka, va = _project(xs, w_qkv_a[layer].astype(BF16), jnp.zeros((attn_w + 2 * kv_a_w,), F32), rope,
                                     (attn_w, kv_a_w, kv_a_w), attn_w + kv_a_w, seq, tn=512,
                                     modes=('zero', 'block', 'one'))
            kmean = _kmean(ka, seq // MOBA_BLOCK)
            attn = _moba_attention(q_pad.reshape(batch, N_KV_A, ga, seq, LANES), ka.transpose(0, 1, 3, 2), va, kmean)
            attn2d = attn.reshape(batch * seq, attn_w)
            wo, bo = w_o_a[layer], jnp.zeros((d,), F32)
        else:
            jb = layer - n_a
            if jb == 0:
                k_pad, v_sh = _project(xs, w_kv_shared.astype(BF16), b_kv_shared, rope, (kv_b_w, kv_b_w), kv_b_w, seq,
                                       tn=256, modes=('zero', 'zero'))
                kt_sh = k_pad.transpose(0, 1, 3, 2)
            (q_pad,) = _project(xs, w_q_b[jb].astype(BF16), b_q_b[jb], rope, (attn_w,), attn_w, seq, tn=512,
                                modes=('zero',))
            attn = _swa_attention(q_pad.reshape(batch, N_KV_B, gb, seq, LANES), kt_sh, v_sh, sinks_b[jb])
            attn2d = attn.reshape(batch * seq, attn_w)
            wo, bo = w_o_b[jb], b_o_b[jb]
        x1, top_idx, gates = _oproj_ln_router(attn2d, xs, wo.astype(BF16), bo, ln_mix_g[layer],
                                              ln_mix_b[layer], w_router[layer], b_router[layer], alpha)
        xs = _moe_layer(x1, top_idx, gates, layer, wg_all, wu_all, b_gate_up[layer], wd_all, b_down[layer],
                        ln_ffn_g[layer], ln_ffn_b[layer], alpha)
    return xs.reshape(batch, seq, d).astype(x.dtype)
```

```python
import functools

import numpy as np
import jax
import jax.numpy as jnp
from jax import lax
from jax.experimental import pallas as pl
from jax.experimental.pallas import tpu as pltpu

F32 = jnp.float32
BF16 = jnp.bfloat16
NEG_INF = float("-inf")

HEAD_DIM = 64
N_HEADS = 32
N_KV_A = 8
N_KV_B = 4
ROPE_DIM = 16
ROPE_THETA = 500000.0
MOBA_BLOCK = 256
MOBA_TOPK = 3
SWA_WINDOW = 128
N_EXPERTS = 32
TOP_K = 4
SWIGLU_LIMIT = 7.0
SWIGLU_ALPHA = 1.702
LN_EPS = 1e-5

MOBA_MASK_BIAS = -(2.0 ** 100)

LANES = 128
SUBLANES = 8
V7X_VMEM_BYTES = 64 * 1024 * 1024
VMEM_LIMIT = V7X_VMEM_BYTES - 8 * 1024 * 1024

MOBA_KV_TILE = 512
TM_PROJ = 512
TM_OPROJ = 512
TM_MOE = 512
TM_COMBINE = 256
HC_MOE = 256
TR_SPLIT = 1024


def _cparams(semantics):
    return pltpu.CompilerParams(dimension_semantics=semantics, vmem_limit_bytes=VMEM_LIMIT)


def _proj_kernel(x_ref, w_ref, b_ref, c_ref, sa_ref, sb_ref, *o_refs, splits, modes, rope_cols, tn, tm, pos_blocks):
    x = x_ref[...].astype(BF16)
    lane = lax.broadcasted_iota(jnp.int32, (tm, LANES), 1)
    lower = lane < HEAD_DIM
    fills = {}
    if 'zero' in modes:
        fills['zero'] = jnp.zeros((tm, LANES), F32)
    if 'one' in modes:
        fills['one'] = jnp.where(lane == HEAD_DIM, 1.0, 0.0)
    if 'block' in modes:
        pos = (pl.program_id(0) % pos_blocks) * tm + lax.broadcasted_iota(jnp.int32, (tm, LANES), 0)
        fills['block'] = jnp.where(lane - HEAD_DIM == pos // MOBA_BLOCK, 1.0, 0.0)
    col = 0
    for o_ref, width, mode in zip(o_refs, splits, modes):
        for c0 in range(0, width, tn):
            lo = col + c0
            y = jnp.dot(x, w_ref[:, lo:lo + tn], preferred_element_type=F32) + b_ref[:, lo:lo + tn]
            for l0 in range(0, tn, LANES):
                seg = y[:, l0:l0 + LANES]
                if lo < rope_cols:
                    seg = (seg * c_ref[...] + pltpu.roll(seg, LANES - ROPE_DIM // 2, 1) * sa_ref[...]
                           + pltpu.roll(seg, ROPE_DIM // 2, 1) * sb_ref[...])
                if mode is None:
                    o_ref[:, c0 + l0:c0 + l0 + LANES] = seg.astype(o_ref.dtype)
                else:
                    head = (c0 + l0) // HEAD_DIM
                    o_ref[head] = jnp.where(lower, seg, fills[mode]).astype(o_ref.dtype)
                    o_ref[head + 1] = jnp.where(lower, pltpu.roll(seg, HEAD_DIM, 1), fills[mode]).astype(o_ref.dtype)
        col += width


def _project(x2d, w_bf16, bias, rope_tabs, splits, rope_cols, seq, tn, modes=None):
    n, d = x2d.shape
    nout = w_bf16.shape[1]
    tm = min(TM_PROJ, seq)
    modes = tuple(modes) if modes is not None else (None,) * len(splits)
    assert n % tm == 0 and seq % tm == 0 and sum(splits) == nout
    assert all(s % tn == 0 for s in splits) and rope_cols % tn == 0 and tn % LANES == 0
    assert 2 * HEAD_DIM == LANES and (tm % MOBA_BLOCK == 0 or 'block' not in modes)
    pos_blocks = seq // tm
    batch = n // seq
    c_tab, sa_tab, sb_tab = rope_tabs
    tab_spec = pl.BlockSpec((tm, LANES), lambda i: (i % pos_blocks, 0))
    out_specs, out_shape = [], []
    for s, mode in zip(splits, modes):
        if mode is None:
            out_specs.append(pl.BlockSpec((tm, s), lambda i: (i, 0)))
            out_shape.append(jax.ShapeDtypeStruct((n, s), BF16))
        else:
            heads = s // HEAD_DIM
            out_specs.append(pl.BlockSpec((None, heads, tm, LANES), lambda i: (i // pos_blocks, 0, i % pos_blocks, 0)))
            out_shape.append(jax.ShapeDtypeStruct((batch, heads, seq, LANES), BF16))
    kern = functools.partial(_proj_kernel, splits=tuple(splits), modes=modes, rope_cols=rope_cols, tn=tn, tm=tm,
                             pos_blocks=pos_blocks)
    return pl.pallas_call(
        kern,
        grid=(n // tm,),
        in_specs=[
            pl.BlockSpec((tm, d), lambda i: (i, 0)),
            pl.BlockSpec((d, nout), lambda i: (0, 0)),
            pl.BlockSpec((1, nout), lambda i: (0, 0)),
            tab_spec, tab_spec, tab_spec,
        ],
        out_specs=out_specs,
        out_shape=out_shape,
        compiler_params=_cparams(("parallel",)),
        name="proj_rope",
    )(x2d, w_bf16, bias.reshape(1, nout).astype(F32), c_tab, sa_tab, sb_tab)


def _rope_tables(seq):
    half = ROPE_DIM // 2
    inv = ROPE_THETA ** (-jnp.arange(0, ROPE_DIM, 2, dtype=F32) / ROPE_DIM)
    ang = jnp.arange(seq, dtype=F32)[:, None] * inv[None, :]
    cos, sin = jnp.cos(ang), jnp.sin(ang)
    ones = jnp.ones((seq, HEAD_DIM - ROPE_DIM), F32)
    zeros = jnp.zeros((seq, HEAD_DIM - ROPE_DIM), F32)
    zh = jnp.zeros((seq, half), F32)
    c_head = jnp.concatenate([cos, cos, ones], axis=1)
    sa_head = jnp.concatenate([-sin, zh, zeros], axis=1)
    sb_head = jnp.concatenate([zh, sin, zeros], axis=1)
    rep = LANES // HEAD_DIM
    return tuple(jnp.tile(t, (1, rep)) for t in (c_head, sa_head, sb_head))


def _kmean_kernel(k_ref, o_ref, *, nb, blk):
    k = k_ref[...].astype(F32)
    km = jnp.mean(k.reshape(nb, blk, LANES), axis=1)
    lane = lax.broadcasted_iota(jnp.int32, (nb, LANES), 1)
    o_ref[...] = jnp.zeros(o_ref.shape, F32)
    o_ref[0:nb, :] = jnp.where(lane < HEAD_DIM, km, 0.0)


def _kmean(ka, nb):
    b, hkv, s, _ = ka.shape
    assert nb <= LANES
    return pl.pallas_call(
        functools.partial(_kmean_kernel, nb=nb, blk=MOBA_BLOCK),
        grid=(b, hkv),
        in_specs=[pl.BlockSpec((None, None, s, LANES), lambda i, j: (i, j, 0, 0))],
        out_specs=pl.BlockSpec((None, None, LANES, LANES), lambda i, j: (i, j, 0, 0)),
        out_shape=jax.ShapeDtypeStruct((b, hkv, LANES, LANES), F32),
        compiler_params=_cparams(("parallel", "parallel")),
        name="moba_kmean",
    )(ka)


def _moba_kernel(it_ref, jt_ref, q_ref, kta_ref, va_ref, km_ref, o_ref, qa_scr, m_scr, acc_scr,
                 *, groups, blk, kv_tile, nbp, scale):
    p = pl.program_id(2)
    i = it_ref[p]
    jt = jt_ref[p]
    own_tile = i // (kv_tile // blk)
    is_diag = jt == own_tile
    rows = groups * blk

    def softmax_step(s, first):
        parts = [s[:, c:c + LANES] for c in range(0, kv_tile, LANES)]
        mx = parts[0]
        for part in parts[1:]:
            mx = jnp.maximum(mx, part)
        rm = jnp.max(mx, axis=-1, keepdims=True)
        if first:
            m_new = jnp.broadcast_to(rm, (rows, LANES))
        else:
            m_old = m_scr[...]
            m_new = jnp.maximum(m_old, rm)
        pr = jnp.concatenate([jnp.exp(part - m_new) for part in parts], axis=1).astype(BF16)
        pv = jnp.dot(pr, va_ref[...], preferred_element_type=F32)
        if first:
            acc_scr[...] = pv
        else:
            acc_scr[...] = jnp.exp(m_old - m_new) * acc_scr[...] + pv
        m_scr[...] = m_new

    @pl.when(is_diag)
    def _():
        qf = q_ref[...].reshape(rows, LANES).astype(F32)
        gate = lax.dot_general(km_ref[0:nbp, :], qf, (((1,), (1,)), ((), ())),
                               precision=lax.Precision.HIGHEST, preferred_element_type=F32)
        kb = lax.broadcasted_iota(jnp.int32, (nbp, rows), 0)
        kb_f = kb.astype(F32)
        cur = jnp.where(kb < i, gate, NEG_INF)
        sel = jnp.zeros((nbp, rows), F32)
        for _ in range(MOBA_TOPK):
            mx = jnp.max(cur, axis=0, keepdims=True)
            idx = jnp.min(jnp.where(cur == mx, kb_f, float(nbp)), axis=0, keepdims=True)
            hit = kb_f == idx
            sel = jnp.where(hit & (mx > NEG_INF), 1.0, sel)
            cur = jnp.where(hit, NEG_INF, cur)
        bias_t = jnp.where((sel > 0.0) | (kb >= i), 0.0, MOBA_MASK_BIAS)
        bias_t = jnp.concatenate([jnp.zeros((HEAD_DIM, rows), F32), bias_t,
                                  jnp.zeros((LANES - HEAD_DIM - nbp, rows), F32)], axis=0)
        qa = (qf * scale + bias_t.T).astype(BF16)
        qa_scr[...] = qa
        s = jnp.dot(qa, kta_ref[...], preferred_element_type=F32)
        qpos = i * blk + lax.broadcasted_iota(jnp.int32, (groups, blk, kv_tile), 1).reshape(rows, kv_tile)
        kpos = jt * kv_tile + lax.broadcasted_iota(jnp.int32, (rows, kv_tile), 1)
        s = jnp.where(kpos <= qpos, s, NEG_INF)
        softmax_step(s, True)

    @pl.when(jnp.logical_not(is_diag))
    def _():
        s = jnp.dot(qa_scr[...], kta_ref[...], preferred_element_type=F32)
        softmax_step(s, False)

    @pl.when((jt == own_tile - 1) | (own_tile == 0))
    def _():
        acc = acc_scr[...]
        o = acc / acc[:, HEAD_DIM:HEAD_DIM + 1]
        lower = lax.broadcasted_iota(jnp.int32, (blk, LANES), 1) < HEAD_DIM
        pairs = [jnp.where(lower, o[g * blk:(g + 1) * blk], pltpu.roll(o[(g + 1) * blk:(g + 2) * blk], HEAD_DIM, 1))
                 for g in range(0, groups, 2)]
        o_ref[...] = jnp.concatenate(pairs, axis=1).astype(o_ref.dtype)


def _moba_attention(q_pad, kta, va, kmean):
    b, hkv, groups, s, _ = q_pad.shape
    blk = MOBA_BLOCK
    assert s % blk == 0 and blk == 2 * LANES
    nb = s // blk
    assert nb <= LANES - HEAD_DIM
    kv_tile = MOBA_KV_TILE if s % MOBA_KV_TILE == 0 else blk
    per_tile = kv_tile // blk
    it, jt = [], []
    for i in range(nb):
        it.append(i)
        jt.append(i // per_tile)
        for j in range(i // per_tile):
            it.append(i)
            jt.append(j)
    it = jnp.asarray(np.asarray(it, np.int32))
    jt = jnp.asarray(np.asarray(jt, np.int32))
    nbp = -(-nb // SUBLANES) * SUBLANES
    assert nbp < LANES - HEAD_DIM
    kern = functools.partial(_moba_kernel, groups=groups, blk=blk, kv_tile=kv_tile, nbp=nbp,
                             scale=HEAD_DIM ** -0.5)
    grid_spec = pltpu.PrefetchScalarGridSpec(
        num_scalar_prefetch=2,
        grid=(b, hkv, int(it.shape[0])),
        in_specs=[
            pl.BlockSpec((None, None, groups, blk, LANES), lambda bi, h, p, it, jt: (bi, h, 0, it[p], 0)),
            pl.BlockSpec((None, None, LANES, kv_tile), lambda bi, h, p, it, jt: (bi, h, 0, jt[p])),
            pl.BlockSpec((None, None, kv_tile, LANES), lambda bi, h, p, it, jt: (bi, h, jt[p], 0)),
            pl.BlockSpec((None, None, LANES, LANES), lambda bi, h, p, it, jt: (bi, h, 0, 0)),
        ],
        out_specs=pl.BlockSpec((None, blk, groups * HEAD_DIM), lambda bi, h, p, it, jt: (bi, it[p], h)),
        scratch_shapes=[
            pltpu.VMEM((groups * blk, LANES), BF16),
            pltpu.VMEM((groups * blk, LANES), F32),
            pltpu.VMEM((groups * blk, LANES), F32),
        ],
    )
    return pl.pallas_call(
        kern,
        grid_spec=grid_spec,
        out_shape=jax.ShapeDtypeStruct((b, s, hkv * groups * HEAD_DIM), BF16),
        compiler_params=_cparams(("parallel", "parallel", "arbitrary")),
        name="moba_attn",
    )(it, jt, q_pad, kta, va, kmean)


def _swa_kernel(q_ref, ktp_ref, ktc_ref, vp_ref, vc_ref, sink_ref, o_ref, *, groups, win, scale):
    kh = pl.program_id(1)
    i = pl.program_id(2)
    row = lax.broadcasted_iota(jnp.int32, (win, win), 0)
    col = lax.broadcasted_iota(jnp.int32, (win, win), 1)
    cur_mask = col <= row
    prev_mask = (col > row) & (i > 0)
    outs = []
    for g in range(groups):
        qg = q_ref[g]
        sp = jnp.dot(qg, ktp_ref[...], preferred_element_type=F32) * scale
        sc = jnp.dot(qg, ktc_ref[...], preferred_element_type=F32) * scale
        sp = jnp.where(prev_mask, sp, NEG_INF)
        sc = jnp.where(cur_mask, sc, NEG_INF)
        sink = sink_ref[kh * groups + g]
        mx = jnp.maximum(jnp.maximum(jnp.max(sp, axis=-1, keepdims=True), jnp.max(sc, axis=-1, keepdims=True)), sink)
        pp = jnp.exp(sp - mx)
        pc = jnp.exp(sc - mx)
        den = jnp.sum(pp, axis=-1, keepdims=True) + jnp.sum(pc, axis=-1, keepdims=True) + jnp.exp(sink - mx)
        o = (jnp.dot(pp.astype(BF16), vp_ref[...], preferred_element_type=F32)
             + jnp.dot(pc.astype(BF16), vc_ref[...], preferred_element_type=F32))
        outs.append(o / den)
    lower = lax.broadcasted_iota(jnp.int32, (win, LANES), 1) < HEAD_DIM
    pairs = [jnp.where(lower, outs[g], pltpu.roll(outs[g + 1], HEAD_DIM, 1)) for g in range(0, groups, 2)]
    o_ref[...] = jnp.concatenate(pairs, axis=1).astype(o_ref.dtype)


def _swa_attention(q_pad, kt_pad, v_pad, sinks):
    b, hkv, groups, s, _ = q_pad.shape
    win = SWA_WINDOW
    assert s % win == 0 and groups % 2 == 0
    nb = s // win
    kern = functools.partial(_swa_kernel, groups=groups, win=win, scale=HEAD_DIM ** -0.5)
    return pl.pallas_call(
        kern,
        grid=(b, hkv, nb),
        in_specs=[
            pl.BlockSpec((None, None, groups, win, LANES), lambda bi, h, i: (bi, h, 0, i, 0)),
            pl.BlockSpec((None, None, LANES, win), lambda bi, h, i: (bi, h, 0, jnp.maximum(i - 1, 0))),
            pl.BlockSpec((None, None, LANES, win), lambda bi, h, i: (bi, h, 0, i)),
            pl.BlockSpec((None, None, win, LANES), lambda bi, h, i: (bi, h, jnp.maximum(i - 1, 0), 0)),
            pl.BlockSpec((None, None, win, LANES), lambda bi, h, i: (bi, h, i, 0)),
            pl.BlockSpec(memory_space=pltpu.SMEM),
        ],
        out_specs=pl.BlockSpec((None, win, groups * HEAD_DIM), lambda bi, h, i: (bi, i, h)),
        out_shape=jax.ShapeDtypeStruct((b, s, hkv * groups * HEAD_DIM), BF16),
        compiler_params=_cparams(("parallel", "parallel", "parallel")),
        name="swa_attn",
    )(q_pad, kt_pad, kt_pad, v_pad, v_pad, sinks.astype(F32))


def _layer_norm_rows(z, g, b):
    mu = jnp.mean(z, axis=-1, keepdims=True)
    zc = z - mu
    var = jnp.mean(zc * zc, axis=-1, keepdims=True)
    return zc * lax.rsqrt(var + LN_EPS) * g + b


def _pack_bf16_pairs(v):
    half = v.shape[1] // 2
    hi = lax.bitcast_convert_type(v[:, :half].astype(BF16).astype(F32), jnp.uint32)
    lo = lax.bitcast_convert_type(v[:, half:].astype(BF16).astype(F32), jnp.uint32)
    return hi | (lo >> 16)


def _unpack_bf16_pairs(u):
    hi = lax.bitcast_convert_type(u & jnp.uint32(0xFFFF0000), F32)
    lo = lax.bitcast_convert_type(u << 16, F32)
    return hi, lo


def _oproj_kernel(a_ref, x_ref, wo_ref, bo_ref, g_ref, b_ref, wrh_ref, wrl_ref, br_ref, x1_ref, xp_ref, ti_ref, tg_ref,
                  *, alpha, n_experts):
    mix = jnp.dot(a_ref[...], wo_ref[...], preferred_element_type=F32) + bo_ref[...]
    x1 = _layer_norm_rows(alpha * x_ref[...] + mix, g_ref[...], b_ref[...])
    x1_ref[...] = x1
    xp_ref[...] = _pack_bf16_pairs(x1)
    xh = x1.astype(BF16)
    xl = (x1 - xh.astype(F32)).astype(BF16)
    logits = (jnp.dot(xh, wrh_ref[...], preferred_element_type=F32)
              + jnp.dot(xh, wrl_ref[...], preferred_element_type=F32)
              + jnp.dot(xl, wrh_ref[...], preferred_element_type=F32)) + br_ref[...]
    lane = lax.broadcasted_iota(jnp.int32, logits.shape, 1)
    cur = jnp.where(lane < n_experts, logits, NEG_INF)
    vals, idxs = [], []
    for _ in range(TOP_K):
        mx = jnp.max(cur, axis=-1, keepdims=True)
        idx = jnp.min(jnp.where(cur == mx, lane, LANES), axis=-1, keepdims=True)
        vals.append(mx)
        idxs.append(idx)
        cur = jnp.where(lane == idx, NEG_INF, cur)
    exps = [jnp.exp(v - vals[0]) for v in vals]
    den = exps[0]
    for e in exps[1:]:
        den = den + e
    ti = jnp.zeros(logits.shape, jnp.int32)
    tg = jnp.zeros(logits.shape, F32)
    for k in range(TOP_K):
        ti = jnp.where(lane == k, idxs[k], ti)
        tg = jnp.where(lane == k, exps[k] / den, tg)
    ti_ref[...] = ti
    tg_ref[...] = tg


def _oproj_ln_router(attn2d, x2d, wo_bf16, bo, ln_g, ln_b, w_r, b_r, alpha):
    n, d = x2d.shape
    da = attn2d.shape[1]
    tm = min(TM_OPROJ, n)
    assert n % tm == 0
    n_experts = w_r.shape[1]
    wr_pad = jnp.zeros((d, LANES), F32).at[:, :n_experts].set(w_r.astype(F32))
    wr_hi = wr_pad.astype(BF16)
    wr_lo = (wr_pad - wr_hi.astype(F32)).astype(BF16)
    br_pad = jnp.zeros((1, LANES), F32).at[0, :n_experts].set(b_r.astype(F32))
    row = lambda i: (i, 0)
    fixed = lambda i: (0, 0)
    return pl.pallas_call(
        functools.partial(_oproj_kernel, alpha=alpha, n_experts=n_experts),
        grid=(n // tm,),
        in_specs=[
            pl.BlockSpec((tm, da), row),
            pl.BlockSpec((tm, d), row),
            pl.BlockSpec((da, d), fixed),
            pl.BlockSpec((1, d), fixed),
            pl.BlockSpec((1, d), fixed),
            pl.BlockSpec((1, d), fixed),
            pl.BlockSpec((d, LANES), fixed),
            pl.BlockSpec((d, LANES), fixed),
            pl.BlockSpec((1, LANES), fixed),
        ],
        out_specs=[pl.BlockSpec((tm, d), row), pl.BlockSpec((tm, d // 2), row), pl.BlockSpec((tm, LANES), row),
                   pl.BlockSpec((tm, LANES), row)],
        out_shape=[jax.ShapeDtypeStruct((n, d), F32), jax.ShapeDtypeStruct((n, d // 2), jnp.uint32),
                   jax.ShapeDtypeStruct((n, LANES), jnp.int32), jax.ShapeDtypeStruct((n, LANES), F32)],
        compiler_params=_cparams(("parallel",)),
        name="oproj_ln_router",
    )(attn2d, x2d, wo_bf16, bo.reshape(1, d).astype(F32), ln_g.reshape(1, d).astype(F32),
      ln_b.reshape(1, d).astype(F32), wr_hi, wr_lo, br_pad)


def _split_gate_up_kernel(w_ref, pg_ref, pu_ref, g_ref, u_ref, *, width):
    for c in range(0, width, 2 * LANES):
        blk = w_ref[:, c:c + 2 * LANES].astype(BF16)
        g_ref[:, c // 2:c // 2 + LANES] = jnp.dot(blk, pg_ref[...], preferred_element_type=F32).astype(BF16)
        u_ref[:, c // 2:c // 2 + LANES] = jnp.dot(blk, pu_ref[...], preferred_element_type=F32).astype(BF16)


def _split_gate_up(w_gate_up):
    nl, ne, d, width = w_gate_up.shape
    assert width % (2 * LANES) == 0
    tr = min(TR_SPLIT, d)
    assert d % tr == 0
    k = jnp.arange(2 * LANES, dtype=jnp.int32)[:, None]
    j = jnp.arange(LANES, dtype=jnp.int32)[None, :]
    pick_gate = (k == 2 * j).astype(BF16)
    pick_lin = (k == 2 * j + 1).astype(BF16)
    out = jax.ShapeDtypeStruct((nl * ne, d, width // 2), BF16)
    return pl.pallas_call(
        functools.partial(_split_gate_up_kernel, width=width),
        grid=(nl * ne, d // tr),
        in_specs=[
            pl.BlockSpec((None, tr, width), lambda e, r: (e, r, 0)),
            pl.BlockSpec((2 * LANES, LANES), lambda e, r: (0, 0)),
            pl.BlockSpec((2 * LANES, LANES), lambda e, r: (0, 0)),
        ],
        out_specs=[pl.BlockSpec((None, tr, width // 2), lambda e, r: (e, r, 0))] * 2,
        out_shape=[out, out],
        compiler_params=_cparams(("parallel", "parallel")),
        name="split_gate_up",
    )(w_gate_up.reshape(nl * ne, d, width), pick_gate, pick_lin)


def _moe_gemm_kernel(be_ref, nu_ref, src_hbm, dst_hbm, x_hbm, wg_ref, wu_ref, bg_ref, bu_ref, wd_ref, bd_ref, y_hbm,
                     gtab, stab, xg0, xg1, yb0, yb1, yacc, gtsem, stsem, gsem, ssem, *, tm, hidden, hc, n_blocks):
    del be_ref
    i = pl.program_id(0)
    n_used = nu_ref[0]
    last = n_used - 1
    xg = (xg0, xg1)
    yb = (yb0, yb1)

    def src_copy(blk, s):
        return pltpu.make_async_copy(src_hbm.at[blk], gtab.at[pl.ds(s, 1)], gtsem.at[s])

    def dst_copy(blk, s):
        return pltpu.make_async_copy(dst_hbm.at[blk], stab.at[pl.ds(s, 1)], stsem.at[s])

    def gather_start(s, r, priority=0):
        tok = gtab[s, r]
        pltpu.make_async_copy(x_hbm.at[pl.ds(tok, 1)], xg[s].at[pl.ds(r, 1)], gsem.at[s]).start(priority)

    def gather_wait(s):
        pltpu.make_async_copy(x_hbm.at[pl.ds(0, tm)], xg[s], gsem.at[s]).wait()

    def scatter_start(s, r, priority=0):
        row = stab[s, r]
        pltpu.make_async_copy(yb[s].at[pl.ds(r, 1)], y_hbm.at[pl.ds(row, 1)], ssem.at[s]).start(priority)

    def scatter_wait(s):
        pltpu.make_async_copy(yb[s], y_hbm.at[pl.ds(0, tm)], ssem.at[s]).wait()

    @pl.when(i == 0)
    def _():
        src_copy(0, 0).start()
        src_copy(jnp.minimum(1, last), 1).start()
        dst_copy(n_blocks, 1).start()
        src_copy(0, 0).wait()

        def body(r, carry):
            gather_start(0, r)
            return carry
        lax.fori_loop(0, tm, body, 0, unroll=8)
        yb1[...] = jnp.zeros(yb1.shape, yb1.dtype)
        tail = y_hbm.shape[0] - 2 * tm
        for half in range(2):
            pltpu.make_async_copy(yb1, y_hbm.at[pl.ds(tail + half * tm, tm)], ssem.at[1]).start()
        for half in range(2):
            pltpu.make_async_copy(yb1, y_hbm.at[pl.ds(tail + half * tm, tm)], ssem.at[1]).wait()

    def step(slot):
        nslot = 1 - slot
        src_copy(jnp.minimum(i + 2, last), slot).start()
        dst_copy(i, slot).start()
        src_copy(0, nslot).wait()
        dst_copy(0, nslot).wait()
        gather_wait(slot)

        @pl.when(i >= 1)
        def _():
            scatter_wait(slot)

        x_hi, x_lo = _unpack_bf16_pairs(xg[slot][...])
        xb = jnp.concatenate([x_hi.astype(BF16), x_lo.astype(BF16)], axis=1)
        n_chunks = hidden // hc
        rows_per_chunk = tm // n_chunks
        for ci in range(n_chunks):
            c = ci * hc
            @pl.when(n_used > 0)
            def _():
                for r in range(ci * rows_per_chunk, (ci + 1) * rows_per_chunk):
                    gather_start(nslot, r, priority=r % 2)
                    scatter_start(nslot, r, priority=r % 2)

            glu = jnp.dot(xb, wg_ref[:, c:c + hc], preferred_element_type=F32) + bg_ref[:, c:c + hc]
            lin = jnp.dot(xb, wu_ref[:, c:c + hc], preferred_element_type=F32) + bu_ref[:, c:c + hc]
            glu = jnp.minimum(glu, SWIGLU_LIMIT)
            lin = jnp.clip(lin, -SWIGLU_LIMIT, SWIGLU_LIMIT)
            act = glu * jax.nn.sigmoid(SWIGLU_ALPHA * glu) * (lin + 1.0)
            part = jnp.dot(act.astype(BF16), wd_ref[c:c + hc, :], preferred_element_type=F32)
            if ci == 0:
                yacc[...] = part + bd_ref[...]
            else:
                yacc[...] += part
        yb[slot][...] = _pack_bf16_pairs(yacc[...])

    def drain(slot):
        nslot = 1 - slot
        src_copy(0, nslot).wait()
        dst_copy(0, nslot).wait()

        def body(r, carry):
            scatter_start(nslot, r)
            return carry
        lax.fori_loop(0, tm, body, 0, unroll=8)
        scatter_wait(slot)
        scatter_wait(nslot)
        gather_wait(slot)

    for parity in range(2):
        @pl.when((i < n_used) & (i % 2 == parity))
        def _():
            step(parity)

        @pl.when((i == n_used) & (i % 2 == parity))
        def _():
            drain(parity)


def _moe_gemm(xp, src_tab, dst_tab, block_expert, n_used, layer, wg, wu, bg, bu, wd, bd, tm):
    n, dp = xp.shape
    d = 2 * dp
    n_blocks = src_tab.shape[0]
    n_experts, hidden = wd.shape[1], wd.shape[2]
    hc = min(HC_MOE, hidden)
    assert hidden % hc == 0 and hc % LANES == 0 and tm % (hidden // hc) == 0
    assert dst_tab.shape[0] == n_blocks + 1 and block_expert.shape[0] == n_blocks + 1
    w_flat = lambda i, be, nu: (layer * n_experts + be[i], 0, 0)
    w_in = lambda i, be, nu: (be[i], 0, 0)
    grid_spec = pltpu.PrefetchScalarGridSpec(
        num_scalar_prefetch=2,
        grid=(n_blocks + 1,),
        in_specs=[
            pl.BlockSpec(memory_space=pl.ANY),
            pl.BlockSpec(memory_space=pl.ANY),
            pl.BlockSpec(memory_space=pl.ANY),
            pl.BlockSpec((None, d, hidden), w_flat),
            pl.BlockSpec((None, d, hidden), w_flat),
            pl.BlockSpec((None, 1, hidden), w_in),
            pl.BlockSpec((None, 1, hidden), w_in),
            pl.BlockSpec((None, None, hidden, d), lambda i, be, nu: (layer, be[i], 0, 0)),
            pl.BlockSpec((None, 1, d), w_in),
        ],
        out_specs=pl.BlockSpec(memory_space=pl.ANY),
        scratch_shapes=[
            pltpu.SMEM((2, tm), jnp.int32),
            pltpu.SMEM((2, tm), jnp.int32),
            pltpu.VMEM((tm, dp), jnp.uint32),
            pltpu.VMEM((tm, dp), jnp.uint32),
            pltpu.VMEM((tm, dp), jnp.uint32),
            pltpu.VMEM((tm, dp), jnp.uint32),
            pltpu.VMEM((tm, d), F32),
            pltpu.SemaphoreType.DMA((2,)),
            pltpu.SemaphoreType.DMA((2,)),
            pltpu.SemaphoreType.DMA((2,)),
            pltpu.SemaphoreType.DMA((2,)),
        ],
    )
    y_rows = n * TOP_K + 2 * tm
    return pl.pallas_call(
        functools.partial(_moe_gemm_kernel, tm=tm, hidden=hidden, hc=hc, n_blocks=n_blocks),
        grid_spec=grid_spec,
        out_shape=jax.ShapeDtypeStruct((y_rows, dp), jnp.uint32),
        compiler_params=_cparams(("arbitrary",)),
        name="moe_gemm",
    )(block_expert, n_used, src_tab, dst_tab, xp, wg, wu, bg, bu, wd, bd)


def _dispatch_tables(top_idx, tm, n_blocks):
    n = top_idx.shape[0]
    nk = n * TOP_K
    e_flat = top_idx.reshape(-1)
    onehot = (e_flat[:, None] == jnp.arange(N_EXPERTS, dtype=jnp.int32)[None, :]).astype(jnp.int32)
    csum = jnp.cumsum(onehot, axis=0)
    counts = csum[-1]
    rank = jnp.sum((csum - onehot) * onehot, axis=1)
    blocks_e = (counts + tm - 1) // tm
    bend = jnp.cumsum(blocks_e)
    bstart = bend - blocks_e
    dest = jnp.sum(onehot * bstart[None, :], axis=1) * tm + rank
    row_pair = jnp.full((n_blocks * tm,), -1, jnp.int32).at[dest].set(jnp.arange(nk, dtype=jnp.int32))
    row_pair = row_pair.reshape(n_blocks, tm)
    blk_ids = jnp.arange(n_blocks + 1, dtype=jnp.int32)
    trash = nk + (blk_ids[:n_blocks, None] % 2) * tm + jnp.arange(tm, dtype=jnp.int32)[None, :]
    src_tab = jnp.where(row_pair >= 0, row_pair // TOP_K, 0).astype(jnp.int32).reshape(n_blocks, 1, tm)
    dst_tab = jnp.where(row_pair >= 0, (row_pair % TOP_K) * n + row_pair // TOP_K, trash)
    before_first = nk + tm + jnp.arange(tm, dtype=jnp.int32)[None, :]
    dst_tab = jnp.concatenate([dst_tab, before_first], axis=0).astype(jnp.int32).reshape(n_blocks + 1, 1, tm)
    block_expert = jnp.minimum(jnp.sum((bend[None, :] <= blk_ids[:, None]).astype(jnp.int32), axis=1),
                               N_EXPERTS - 1).astype(jnp.int32)
    n_used = bend[-1:].astype(jnp.int32)
    return src_tab, dst_tab, block_expert, n_used


def _combine_kernel(*refs, alpha):
    y_refs, (x_ref, tg_ref, g_ref, b_ref, o_ref) = refs[:TOP_K], refs[TOP_K:]
    tg = tg_ref[...]
    y = None
    for k in range(TOP_K):
        yk = jnp.concatenate(_unpack_bf16_pairs(y_refs[k][...]), axis=1)
        y = tg[:, k:k + 1] * yk if y is None else y + tg[:, k:k + 1] * yk
    o_ref[...] = _layer_norm_rows(alpha * x_ref[...] + y, g_ref[...], b_ref[...])


def _combine_ln(y4, x1, gates, ln_g, ln_b, alpha):
    n, d = x1.shape
    tm = min(TM_COMBINE, n)
    assert n % tm == 0
    row = lambda i: (i, 0)
    fixed = lambda i: (0, 0)
    return pl.pallas_call(
        functools.partial(_combine_kernel, alpha=alpha),
        grid=(n // tm,),
        in_specs=[pl.BlockSpec((tm, d // 2), functools.partial(lambda k, i: (k * (n // tm) + i, 0), k))
                  for k in range(TOP_K)] + [
            pl.BlockSpec((tm, d), row),
            pl.BlockSpec((tm, LANES), row),
            pl.BlockSpec((1, d), fixed),
            pl.BlockSpec((1, d), fixed),
        ],
        out_specs=pl.BlockSpec((tm, d), row),
        out_shape=jax.ShapeDtypeStruct((n, d), F32),
        compiler_params=_cparams(("parallel",)),
        name="moe_combine_ln",
    )(*([y4] * TOP_K), x1, gates, ln_g.reshape(1, d).astype(F32), ln_b.reshape(1, d).astype(F32))


def _moe_layer(x1, xp, top_idx, gates, layer, wg, wu, b_gate_up, wd, b_down, ln_g, ln_b, alpha):
    n, d = x1.shape
    tm = min(TM_MOE, n)
    n_blocks = -(-(n * TOP_K) // tm) + N_EXPERTS
    src_tab, dst_tab, block_expert, n_used = _dispatch_tables(top_idx[:, :TOP_K], tm, n_blocks)
    hidden = wd.shape[2]
    bg = b_gate_up[:, 0::2].reshape(N_EXPERTS, 1, hidden).astype(F32)
    bu = b_gate_up[:, 1::2].reshape(N_EXPERTS, 1, hidden).astype(F32)
    bd = b_down.reshape(N_EXPERTS, 1, d).astype(F32)
    y4 = _moe_gemm(xp, src_tab, dst_tab, block_expert, n_used, layer, wg, wu, bg, bu, wd, bd, tm)
    return _combine_ln(y4, x1, gates, ln_g, ln_b, alpha)


def kernel(x, w_qkv_a, w_o_a, w_q_b, b_q_b, sinks_b, w_o_b, b_o_b, w_kv_shared, b_kv_shared, ln_mix_g, ln_mix_b, ln_ffn_g, ln_ffn_b, w_router, b_router, w_gate_up, b_gate_up, w_down, b_down):
    batch, seq, d = x.shape
    depth = ln_mix_g.shape[0]
    n_a = w_qkv_a.shape[0]
    alpha = (2 * depth) ** 0.25
    attn_w = N_HEADS * HEAD_DIM
    kv_a_w = N_KV_A * HEAD_DIM
    kv_b_w = N_KV_B * HEAD_DIM
    ga, gb = N_HEADS // N_KV_A, N_HEADS // N_KV_B
    rope = _rope_tables(seq)
    xs = x.reshape(batch * seq, d).astype(F32)
    kt_sh = v_sh = None
    wg_all, wu_all = _split_gate_up(w_gate_up)
    wd_all = w_down.astype(BF16)
    for layer in range(depth):
        if layer < n_a:
            q_pad, ka, va = _project(xs, w_qkv_a[layer].astype(BF16), jnp.zeros((attn_w + 2 * kv_a_w,), F32), rope,
                                     (attn_w, kv_a_w, kv_a_w), attn_w + kv_a_w, seq, tn=512,
                                     modes=('zero', 'block', 'one'))
            kmean = _kmean(ka, seq // MOBA_BLOCK)
            attn = _moba_attention(q_pad.reshape(batch, N_KV_A, ga, seq, LANES), ka.transpose(0, 1, 3, 2), va, kmean)
            attn2d = attn.reshape(batch * seq, attn_w)
            wo, bo = w_o_a[layer], jnp.zeros((d,), F32)
        else:
            jb = layer - n_a
            if jb == 0:
                k_pad, v_sh = _project(xs, w_kv_shared.astype(BF16), b_kv_shared, rope, (kv_b_w, kv_b_w), kv_b_w, seq,
                                       tn=256, modes=('zero', 'zero'))
                kt_sh = k_pad.transpose(0, 1, 3, 2)
            (q_pad,) = _project(xs, w_q_b[jb].astype(BF16), b_q_b[jb], rope, (attn_w,), attn_w, seq, tn=512,
                                modes=('zero',))
            attn = _swa_attention(q_pad.reshape(batch, N_KV_B, gb, seq, LANES), kt_sh, v_sh, sinks_b[jb])
            attn2d = attn.reshape(batch * seq, attn_w)
            wo, bo = w_o_b[jb], b_o_b[jb]
        x1, xp, top_idx, gates = _oproj_ln_router(attn2d, xs, wo.astype(BF16), bo, ln_mix_g[layer],
                                                  ln_mix_b[layer], w_router[layer], b_router[layer], alpha)
        xs = _moe_layer(x1, xp, top_idx, gates, layer, wg_all, wu_all, b_gate_up[layer], wd_all, b_down[layer],
                        ln_ffn_g[layer], ln_ffn_b[layer], alpha)
    return xs.reshape(batch, seq, d).astype(x.dtype)
```

```python
import functools

import numpy as np
import jax
import jax.numpy as jnp
from jax import lax
from jax.experimental import pallas as pl
from jax.experimental.pallas import tpu as pltpu

F32 = jnp.float32
BF16 = jnp.bfloat16
NEG_INF = float("-inf")

HEAD_DIM = 64
N_HEADS = 32
N_KV_A = 8
N_KV_B = 4
ROPE_DIM = 16
ROPE_THETA = 500000.0
MOBA_BLOCK = 256
MOBA_TOPK = 3
SWA_WINDOW = 128
N_EXPERTS = 32
TOP_K = 4
SWIGLU_LIMIT = 7.0
SWIGLU_ALPHA = 1.702
LN_EPS = 1e-5

MOBA_MASK_BIAS = -(2.0 ** 100)

LANES = 128
SUBLANES = 8
V7X_VMEM_BYTES = 64 * 1024 * 1024
VMEM_LIMIT = V7X_VMEM_BYTES - 8 * 1024 * 1024

MOBA_KV_TILE = 512
TM_PROJ = 512
TM_OPROJ = 512
TM_MOE = 512
TM_COMBINE = 256
HC_MOE = 256
TR_SPLIT = 1024


def _cparams(semantics):
    return pltpu.CompilerParams(dimension_semantics=semantics, vmem_limit_bytes=VMEM_LIMIT)


def _proj_kernel(x_ref, w_ref, b_ref, c_ref, sa_ref, sb_ref, *o_refs, splits, modes, rope_cols, tn, tm, pos_blocks):
    x = x_ref[...].astype(BF16)
    lane = lax.broadcasted_iota(jnp.int32, (tm, LANES), 1)
    lower = lane < HEAD_DIM
    fills = {}
    if 'zero' in modes:
        fills['zero'] = jnp.zeros((tm, LANES), F32)
    if 'one' in modes:
        fills['one'] = jnp.where(lane == HEAD_DIM, 1.0, 0.0)
    if 'block' in modes:
        pos = (pl.program_id(0) % pos_blocks) * tm + lax.broadcasted_iota(jnp.int32, (tm, LANES), 0)
        fills['block'] = jnp.where(lane - HEAD_DIM == pos // MOBA_BLOCK, 1.0, 0.0)
    col = 0
    for o_ref, width, mode in zip(o_refs, splits, modes):
        for c0 in range(0, width, tn):
            lo = col + c0
            y = jnp.dot(x, w_ref[:, lo:lo + tn], preferred_element_type=F32) + b_ref[:, lo:lo + tn]
            for l0 in range(0, tn, LANES):
                seg = y[:, l0:l0 + LANES]
                if lo < rope_cols:
                    seg = (seg * c_ref[...] + pltpu.roll(seg, LANES - ROPE_DIM // 2, 1) * sa_ref[...]
                           + pltpu.roll(seg, ROPE_DIM // 2, 1) * sb_ref[...])
                if mode is None:
                    o_ref[:, c0 + l0:c0 + l0 + LANES] = seg.astype(o_ref.dtype)
                else:
                    head = (c0 + l0) // HEAD_DIM
                    o_ref[head] = jnp.where(lower, seg, fills[mode]).astype(o_ref.dtype)
                    o_ref[head + 1] = jnp.where(lower, pltpu.roll(seg, HEAD_DIM, 1), fills[mode]).astype(o_ref.dtype)
        col += width


def _project(x2d, w_bf16, bias, rope_tabs, splits, rope_cols, seq, tn, modes=None):
    n, d = x2d.shape
    nout = w_bf16.shape[1]
    tm = min(TM_PROJ, seq)
    modes = tuple(modes) if modes is not None else (None,) * len(splits)
    assert n % tm == 0 and seq % tm == 0 and sum(splits) == nout
    assert all(s % tn == 0 for s in splits) and rope_cols % tn == 0 and tn % LANES == 0
    assert 2 * HEAD_DIM == LANES and (tm % MOBA_BLOCK == 0 or 'block' not in modes)
    pos_blocks = seq // tm
    batch = n // seq
    c_tab, sa_tab, sb_tab = rope_tabs
    tab_spec = pl.BlockSpec((tm, LANES), lambda i: (i % pos_blocks, 0))
    out_specs, out_shape = [], []
    for s, mode in zip(splits, modes):
        if mode is None:
            out_specs.append(pl.BlockSpec((tm, s), lambda i: (i, 0)))
            out_shape.append(jax.ShapeDtypeStruct((n, s), BF16))
        else:
            heads = s // HEAD_DIM
            out_specs.append(pl.BlockSpec((None, heads, tm, LANES), lambda i: (i // pos_blocks, 0, i % pos_blocks, 0)))
            out_shape.append(jax.ShapeDtypeStruct((batch, heads, seq, LANES), BF16))
    kern = functools.partial(_proj_kernel, splits=tuple(splits), modes=modes, rope_cols=rope_cols, tn=tn, tm=tm,
                             pos_blocks=pos_blocks)
    return pl.pallas_call(
        kern,
        grid=(n // tm,),
        in_specs=[
            pl.BlockSpec((tm, d), lambda i: (i, 0)),
            pl.BlockSpec((d, nout), lambda i: (0, 0)),
            pl.BlockSpec((1, nout), lambda i: (0, 0)),
            tab_spec, tab_spec, tab_spec,
        ],
        out_specs=out_specs,
        out_shape=out_shape,
        compiler_params=_cparams(("parallel",)),
        name="proj_rope",
    )(x2d, w_bf16, bias.reshape(1, nout).astype(F32), c_tab, sa_tab, sb_tab)


def _rope_tables(seq):
    half = ROPE_DIM // 2
    inv = ROPE_THETA ** (-jnp.arange(0, ROPE_DIM, 2, dtype=F32) / ROPE_DIM)
    ang = jnp.arange(seq, dtype=F32)[:, None] * inv[None, :]
    cos, sin = jnp.cos(ang), jnp.sin(ang)
    ones = jnp.ones((seq, HEAD_DIM - ROPE_DIM), F32)
    zeros = jnp.zeros((seq, HEAD_DIM - ROPE_DIM), F32)
    zh = jnp.zeros((seq, half), F32)
    c_head = jnp.concatenate([cos, cos, ones], axis=1)
    sa_head = jnp.concatenate([-sin, zh, zeros], axis=1)
    sb_head = jnp.concatenate([zh, sin, zeros], axis=1)
    rep = LANES // HEAD_DIM
    return tuple(jnp.tile(t, (1, rep)) for t in (c_head, sa_head, sb_head))


def _kmean_kernel(k_ref, o_ref, *, nb, blk):
    k = k_ref[...].astype(F32)
    km = jnp.mean(k.reshape(nb, blk, LANES), axis=1)
    lane = lax.broadcasted_iota(jnp.int32, (nb, LANES), 1)
    o_ref[...] = jnp.zeros(o_ref.shape, F32)
    o_ref[0:nb, :] = jnp.where(lane < HEAD_DIM, km, 0.0)


def _kmean(ka, nb):
    b, hkv, s, _ = ka.shape
    assert nb <= LANES
    return pl.pallas_call(
        functools.partial(_kmean_kernel, nb=nb, blk=MOBA_BLOCK),
        grid=(b, hkv),
        in_specs=[pl.BlockSpec((None, None, s, LANES), lambda i, j: (i, j, 0, 0))],
        out_specs=pl.BlockSpec((None, None, LANES, LANES), lambda i, j: (i, j, 0, 0)),
        out_shape=jax.ShapeDtypeStruct((b, hkv, LANES, LANES), F32),
        compiler_params=_cparams(("parallel", "parallel")),
        name="moba_kmean",
    )(ka)


def _moba_kernel(it_ref, jt_ref, q_ref, kta_ref, va_ref, km_ref, o_ref, qa_scr, m_scr, acc_scr,
                 *, groups, blk, kv_tile, nbp, scale):
    p = pl.program_id(2)
    i = it_ref[p]
    jt = jt_ref[p]
    own_tile = i // (kv_tile // blk)
    is_diag = jt == own_tile
    rows = groups * blk

    def softmax_step(s, first):
        parts = [s[:, c:c + LANES] for c in range(0, kv_tile, LANES)]
        mx = parts[0]
        for part in parts[1:]:
            mx = jnp.maximum(mx, part)
        rm = jnp.max(mx, axis=-1, keepdims=True)
        if first:
            m_new = jnp.broadcast_to(rm, (rows, LANES))
        else:
            m_old = m_scr[...]
            m_new = jnp.maximum(m_old, rm)
        pr = jnp.concatenate([jnp.exp(part - m_new) for part in parts], axis=1).astype(BF16)
        pv = jnp.dot(pr, va_ref[...], preferred_element_type=F32)
        if first:
            acc_scr[...] = pv
        else:
            acc_scr[...] = jnp.exp(m_old - m_new) * acc_scr[...] + pv
        m_scr[...] = m_new

    @pl.when(is_diag)
    def _():
        qf = q_ref[...].reshape(rows, LANES).astype(F32)
        gate = lax.dot_general(km_ref[0:nbp, :], qf, (((1,), (1,)), ((), ())),
                               precision=lax.Precision.HIGHEST, preferred_element_type=F32)
        kb = lax.broadcasted_iota(jnp.int32, (nbp, rows), 0)
        kb_f = kb.astype(F32)
        cur = jnp.where(kb < i, gate, NEG_INF)
        sel = jnp.zeros((nbp, rows), F32)
        for _ in range(MOBA_TOPK):
            mx = jnp.max(cur, axis=0, keepdims=True)
            idx = jnp.min(jnp.where(cur == mx, kb_f, float(nbp)), axis=0, keepdims=True)
            hit = kb_f == idx
            sel = jnp.where(hit & (mx > NEG_INF), 1.0, sel)
            cur = jnp.where(hit, NEG_INF, cur)
        bias_t = jnp.where((sel > 0.0) | (kb >= i), 0.0, MOBA_MASK_BIAS)
        bias_t = jnp.concatenate([jnp.zeros((HEAD_DIM, rows), F32), bias_t,
                                  jnp.zeros((LANES - HEAD_DIM - nbp, rows), F32)], axis=0)
        qa = (qf * scale + bias_t.T).astype(BF16)
        qa_scr[...] = qa
        s = jnp.dot(qa, kta_ref[...], preferred_element_type=F32)
        qpos = i * blk + lax.broadcasted_iota(jnp.int32, (groups, blk, kv_tile), 1).reshape(rows, kv_tile)
        kpos = jt * kv_tile + lax.broadcasted_iota(jnp.int32, (rows, kv_tile), 1)
        s = jnp.where(kpos <= qpos, s, NEG_INF)
        softmax_step(s, True)

    @pl.when(jnp.logical_not(is_diag))
    def _():
        s = jnp.dot(qa_scr[...], kta_ref[...], preferred_element_type=F32)
        softmax_step(s, False)

    @pl.when((jt == own_tile - 1) | (own_tile == 0))
    def _():
        acc = acc_scr[...]
        o = acc / acc[:, HEAD_DIM:HEAD_DIM + 1]
        lower = lax.broadcasted_iota(jnp.int32, (blk, LANES), 1) < HEAD_DIM
        pairs = [jnp.where(lower, o[g * blk:(g + 1) * blk], pltpu.roll(o[(g + 1) * blk:(g + 2) * blk], HEAD_DIM, 1))
                 for g in range(0, groups, 2)]
        o_ref[...] = jnp.concatenate(pairs, axis=1).astype(o_ref.dtype)


def _moba_attention(q_pad, kta, va, kmean):
    b, hkv, groups, s, _ = q_pad.shape
    blk = MOBA_BLOCK
    assert s % blk == 0 and blk == 2 * LANES
    nb = s // blk
    assert nb <= LANES - HEAD_DIM
    kv_tile = MOBA_KV_TILE if s % MOBA_KV_TILE == 0 else blk
    per_tile = kv_tile // blk
    it, jt = [], []
    for i in range(nb):
        it.append(i)
        jt.append(i // per_tile)
        for j in range(i // per_tile):
            it.append(i)
            jt.append(j)
    it = jnp.asarray(np.asarray(it, np.int32))
    jt = jnp.asarray(np.asarray(jt, np.int32))
    nbp = -(-nb // SUBLANES) * SUBLANES
    assert nbp < LANES - HEAD_DIM
    kern = functools.partial(_moba_kernel, groups=groups, blk=blk, kv_tile=kv_tile, nbp=nbp,
                             scale=HEAD_DIM ** -0.5)
    grid_spec = pltpu.PrefetchScalarGridSpec(
        num_scalar_prefetch=2,
        grid=(b, hkv, int(it.shape[0])),
        in_specs=[
            pl.BlockSpec((None, None, groups, blk, LANES), lambda bi, h, p, it, jt: (bi, h, 0, it[p], 0)),
            pl.BlockSpec((None, None, LANES, kv_tile), lambda bi, h, p, it, jt: (bi, h, 0, jt[p])),
            pl.BlockSpec((None, None, kv_tile, LANES), lambda bi, h, p, it, jt: (bi, h, jt[p], 0)),
            pl.BlockSpec((None, None, LANES, LANES), lambda bi, h, p, it, jt: (bi, h, 0, 0)),
        ],
        out_specs=pl.BlockSpec((None, blk, groups * HEAD_DIM), lambda bi, h, p, it, jt: (bi, it[p], h)),
        scratch_shapes=[
            pltpu.VMEM((groups * blk, LANES), BF16),
            pltpu.VMEM((groups * blk, LANES), F32),
            pltpu.VMEM((groups * blk, LANES), F32),
        ],
    )
    return pl.pallas_call(
        kern,
        grid_spec=grid_spec,
        out_shape=jax.ShapeDtypeStruct((b, s, hkv * groups * HEAD_DIM), BF16),
        compiler_params=_cparams(("parallel", "parallel", "arbitrary")),
        name="moba_attn",
    )(it, jt, q_pad, kta, va, kmean)


def _swa_kernel(q_ref, ktp_ref, ktc_ref, vp_ref, vc_ref, sink_ref, o_ref, *, groups, win, scale):
    kh = pl.program_id(1)
    i = pl.program_id(2)
    row = lax.broadcasted_iota(jnp.int32, (win, win), 0)
    col = lax.broadcasted_iota(jnp.int32, (win, win), 1)
    cur_mask = col <= row
    prev_mask = (col > row) & (i > 0)
    outs = []
    for g in range(groups):
        qg = q_ref[g]
        sp = jnp.dot(qg, ktp_ref[...], preferred_element_type=F32) * scale
        sc = jnp.dot(qg, ktc_ref[...], preferred_element_type=F32) * scale
        sp = jnp.where(prev_mask, sp, NEG_INF)
        sc = jnp.where(cur_mask, sc, NEG_INF)
        sink = sink_ref[kh * groups + g]
        mx = jnp.maximum(jnp.maximum(jnp.max(sp, axis=-1, keepdims=True), jnp.max(sc, axis=-1, keepdims=True)), sink)
        pp = jnp.exp(sp - mx)
        pc = jnp.exp(sc - mx)
        den = jnp.sum(pp, axis=-1, keepdims=True) + jnp.sum(pc, axis=-1, keepdims=True) + jnp.exp(sink - mx)
        o = (jnp.dot(pp.astype(BF16), vp_ref[...], preferred_element_type=F32)
             + jnp.dot(pc.astype(BF16), vc_ref[...], preferred_element_type=F32))
        outs.append(o / den)
    lower = lax.broadcasted_iota(jnp.int32, (win, LANES), 1) < HEAD_DIM
    pairs = [jnp.where(lower, outs[g], pltpu.roll(outs[g + 1], HEAD_DIM, 1)) for g in range(0, groups, 2)]
    o_ref[...] = jnp.concatenate(pairs, axis=1).astype(o_ref.dtype)


def _swa_attention(q_pad, kt_pad, v_pad, sinks):
    b, hkv, groups, s, _ = q_pad.shape
    win = SWA_WINDOW
    assert s % win == 0 and groups % 2 == 0
    nb = s // win
    kern = functools.partial(_swa_kernel, groups=groups, win=win, scale=HEAD_DIM ** -0.5)
    return pl.pallas_call(
        kern,
        grid=(b, hkv, nb),
        in_specs=[
            pl.BlockSpec((None, None, groups, win, LANES), lambda bi, h, i: (bi, h, 0, i, 0)),
            pl.BlockSpec((None, None, LANES, win), lambda bi, h, i: (bi, h, 0, jnp.maximum(i - 1, 0))),
            pl.BlockSpec((None, None, LANES, win), lambda bi, h, i: (bi, h, 0, i)),
            pl.BlockSpec((None, None, win, LANES), lambda bi, h, i: (bi, h, jnp.maximum(i - 1, 0), 0)),
            pl.BlockSpec((None, None, win, LANES), lambda bi, h, i: (bi, h, i, 0)),
            pl.BlockSpec(memory_space=pltpu.SMEM),
        ],
        out_specs=pl.BlockSpec((None, win, groups * HEAD_DIM), lambda bi, h, i: (bi, i, h)),
        out_shape=jax.ShapeDtypeStruct((b, s, hkv * groups * HEAD_DIM), BF16),
        compiler_params=_cparams(("parallel", "parallel", "parallel")),
        name="swa_attn",
    )(q_pad, kt_pad, kt_pad, v_pad, v_pad, sinks.astype(F32))


def _layer_norm_rows(z, g, b):
    mu = jnp.mean(z, axis=-1, keepdims=True)
    zc = z - mu
    var = jnp.mean(zc * zc, axis=-1, keepdims=True)
    return zc * lax.rsqrt(var + LN_EPS) * g + b


def _pack_bf16_pairs(v):
    half = v.shape[1] // 2
    hi = lax.bitcast_convert_type(v[:, :half].astype(BF16).astype(F32), jnp.uint32)
    lo = lax.bitcast_convert_type(v[:, half:].astype(BF16).astype(F32), jnp.uint32)
    return hi | (lo >> 16)


def _unpack_bf16_pairs(u):
    hi = lax.bitcast_convert_type(u & jnp.uint32(0xFFFF0000), F32)
    lo = lax.bitcast_convert_type(u << 16, F32)
    return hi, lo


def _oproj_kernel(a_ref, x_ref, wo_ref, bo_ref, g_ref, b_ref, wrh_ref, wrl_ref, br_ref, x1_ref, xp_ref, ti_ref, tg_ref,
                  *, alpha, n_experts):
    mix = jnp.dot(a_ref[...], wo_ref[...], preferred_element_type=F32) + bo_ref[...]
    x1 = _layer_norm_rows(alpha * x_ref[...] + mix, g_ref[...], b_ref[...])
    x1_ref[...] = x1
    xp_ref[...] = _pack_bf16_pairs(x1)
    xh = x1.astype(BF16)
    xl = (x1 - xh.astype(F32)).astype(BF16)
    logits = (jnp.dot(xh, wrh_ref[...], preferred_element_type=F32)
              + jnp.dot(xh, wrl_ref[...], preferred_element_type=F32)
              + jnp.dot(xl, wrh_ref[...], preferred_element_type=F32)) + br_ref[...]
    lane = lax.broadcasted_iota(jnp.int32, logits.shape, 1)
    cur = jnp.where(lane < n_experts, logits, NEG_INF)
    vals, idxs = [], []
    for _ in range(TOP_K):
        mx = jnp.max(cur, axis=-1, keepdims=True)
        idx = jnp.min(jnp.where(cur == mx, lane, LANES), axis=-1, keepdims=True)
        vals.append(mx)
        idxs.append(idx)
        cur = jnp.where(lane == idx, NEG_INF, cur)
    exps = [jnp.exp(v - vals[0]) for v in vals]
    den = exps[0]
    for e in exps[1:]:
        den = den + e
    ti = jnp.zeros(logits.shape, jnp.int32)
    tg = jnp.zeros(logits.shape, F32)
    for k in range(TOP_K):
        ti = jnp.where(lane == k, idxs[k], ti)
        tg = jnp.where(lane == k, exps[k] / den, tg)
    ti_ref[...] = ti
    tg_ref[...] = tg


def _oproj_ln_router(attn2d, x2d, wo_bf16, bo, ln_g, ln_b, w_r, b_r, alpha):
    n, d = x2d.shape
    da = attn2d.shape[1]
    tm = min(TM_OPROJ, n)
    assert n % tm == 0
    n_experts = w_r.shape[1]
    wr_pad = jnp.zeros((d, LANES), F32).at[:, :n_experts].set(w_r.astype(F32))
    wr_hi = wr_pad.astype(BF16)
    wr_lo = (wr_pad - wr_hi.astype(F32)).astype(BF16)
    br_pad = jnp.zeros((1, LANES), F32).at[0, :n_experts].set(b_r.astype(F32))
    row = lambda i: (i, 0)
    fixed = lambda i: (0, 0)
    return pl.pallas_call(
        functools.partial(_oproj_kernel, alpha=alpha, n_experts=n_experts),
        grid=(n // tm,),
        in_specs=[
            pl.BlockSpec((tm, da), row),
            pl.BlockSpec((tm, d), row),
            pl.BlockSpec((da, d), fixed),
            pl.BlockSpec((1, d), fixed),
            pl.BlockSpec((1, d), fixed),
            pl.BlockSpec((1, d), fixed),
            pl.BlockSpec((d, LANES), fixed),
            pl.BlockSpec((d, LANES), fixed),
            pl.BlockSpec((1, LANES), fixed),
        ],
        out_specs=[pl.BlockSpec((tm, d), row), pl.BlockSpec((tm, d // 2), row), pl.BlockSpec((tm, LANES), row),
                   pl.BlockSpec((tm, LANES), row)],
        out_shape=[jax.ShapeDtypeStruct((n, d), F32), jax.ShapeDtypeStruct((n, d // 2), jnp.uint32),
                   jax.ShapeDtypeStruct((n, LANES), jnp.int32), jax.ShapeDtypeStruct((n, LANES), F32)],
        compiler_params=_cparams(("parallel",)),
        name="oproj_ln_router",
    )(attn2d, x2d, wo_bf16, bo.reshape(1, d).astype(F32), ln_g.reshape(1, d).astype(F32),
      ln_b.reshape(1, d).astype(F32), wr_hi, wr_lo, br_pad)


def _split_gate_up_kernel(w_ref, pg_ref, pu_ref, g_ref, u_ref, *, width):
    for c in range(0, width, 2 * LANES):
        blk = w_ref[:, c:c + 2 * LANES].astype(BF16)
        g_ref[:, c // 2:c // 2 + LANES] = jnp.dot(blk, pg_ref[...], preferred_element_type=F32).astype(BF16)
        u_ref[:, c // 2:c // 2 + LANES] = jnp.dot(blk, pu_ref[...], preferred_element_type=F32).astype(BF16)


def _split_gate_up(w_gate_up):
    nl, ne, d, width = w_gate_up.shape
    assert width % (2 * LANES) == 0
    tr = min(TR_SPLIT, d)
    assert d % tr == 0
    k = jnp.arange(2 * LANES, dtype=jnp.int32)[:, None]
    j = jnp.arange(LANES, dtype=jnp.int32)[None, :]
    pick_gate = (k == 2 * j).astype(BF16)
    pick_lin = (k == 2 * j + 1).astype(BF16)
    out = jax.ShapeDtypeStruct((nl * ne, d, width // 2), BF16)
    return pl.pallas_call(
        functools.partial(_split_gate_up_kernel, width=width),
        grid=(nl * ne, d // tr),
        in_specs=[
            pl.BlockSpec((None, tr, width), lambda e, r: (e, r, 0)),
            pl.BlockSpec((2 * LANES, LANES), lambda e, r: (0, 0)),
            pl.BlockSpec((2 * LANES, LANES), lambda e, r: (0, 0)),
        ],
        out_specs=[pl.BlockSpec((None, tr, width // 2), lambda e, r: (e, r, 0))] * 2,
        out_shape=[out, out],
        compiler_params=_cparams(("parallel", "parallel")),
        name="split_gate_up",
    )(w_gate_up.reshape(nl * ne, d, width), pick_gate, pick_lin)


def _moe_gemm_kernel(be_ref, nu_ref, src_hbm, dst_hbm, x_hbm, wg_ref, wu_ref, bg_ref, bu_ref, wd_ref, bd_ref, y_hbm,
                     gtab, stab, xg0, xg1, yb0, yb1, yacc, gtsem, stsem, gsem, ssem, *, tm, hidden, hc, n_blocks):
    del be_ref
    i = pl.program_id(0)
    n_used = nu_ref[0]
    last = n_used - 1
    xg = (xg0, xg1)
    yb = (yb0, yb1)

    def src_copy(blk, s):
        return pltpu.make_async_copy(src_hbm.at[blk], gtab.at[pl.ds(s, 1)], gtsem.at[s])

    def dst_copy(blk, s):
        return pltpu.make_async_copy(dst_hbm.at[blk], stab.at[pl.ds(s, 1)], stsem.at[s])

    def gather_start(s, r, priority=0):
        tok = gtab[s, r]
        pltpu.make_async_copy(x_hbm.at[pl.ds(tok, 1)], xg[s].at[pl.ds(r, 1)], gsem.at[s]).start(priority)

    def gather_wait(s):
        pltpu.make_async_copy(x_hbm.at[pl.ds(0, tm)], xg[s], gsem.at[s]).wait()

    def scatter_start(s, r, priority=0):
        row = stab[s, r]
        pltpu.make_async_copy(yb[s].at[pl.ds(r, 1)], y_hbm.at[pl.ds(row, 1)], ssem.at[s]).start(priority)

    def scatter_wait(s):
        pltpu.make_async_copy(yb[s], y_hbm.at[pl.ds(0, tm)], ssem.at[s]).wait()

    @pl.when(i == 0)
    def _():
        src_copy(0, 0).start()
        src_copy(jnp.minimum(1, last), 1).start()
        dst_copy(n_blocks, 1).start()
        src_copy(0, 0).wait()

        def body(r, carry):
            gather_start(0, r)
            return carry
        lax.fori_loop(0, tm, body, 0, unroll=8)
        yb1[...] = jnp.zeros(yb1.shape, yb1.dtype)
        tail = y_hbm.shape[0] - 2 * tm
        for half in range(2):
            pltpu.make_async_copy(yb1, y_hbm.at[pl.ds(tail + half * tm, tm)], ssem.at[1]).start()
        for half in range(2):
            pltpu.make_async_copy(yb1, y_hbm.at[pl.ds(tail + half * tm, tm)], ssem.at[1]).wait()

    def step(slot):
        nslot = 1 - slot
        src_copy(jnp.minimum(i + 2, last), slot).start()
        dst_copy(i, slot).start()
        src_copy(0, nslot).wait()
        dst_copy(0, nslot).wait()
        gather_wait(slot)

        @pl.when(i >= 1)
        def _():
            scatter_wait(slot)

        x_hi, x_lo = _unpack_bf16_pairs(xg[slot][...])
        xb = jnp.concatenate([x_hi.astype(BF16), x_lo.astype(BF16)], axis=1)
        n_chunks = hidden // hc
        rows_per_chunk = tm // n_chunks
        for ci in range(n_chunks):
            c = ci * hc
            r0 = ci * rows_per_chunk

            @pl.when(n_used > 0)
            def _():
                for r in range(r0, r0 + rows_per_chunk):
                    gather_start(nslot, r, priority=r % 2)

            glu = jnp.dot(xb, wg_ref[:, c:c + hc], preferred_element_type=F32) + bg_ref[:, c:c + hc]
            lin = jnp.dot(xb, wu_ref[:, c:c + hc], preferred_element_type=F32) + bu_ref[:, c:c + hc]
            glu = jnp.minimum(glu, SWIGLU_LIMIT)
            lin = jnp.clip(lin, -SWIGLU_LIMIT, SWIGLU_LIMIT)
            act = glu * jax.nn.sigmoid(SWIGLU_ALPHA * glu) * (lin + 1.0)

            @pl.when(n_used > 0)
            def _():
                for r in range(r0, r0 + rows_per_chunk):
                    scatter_start(nslot, r, priority=r % 2)

            part = jnp.dot(act.astype(BF16), wd_ref[c:c + hc, :].astype(BF16), preferred_element_type=F32)
            if ci == 0:
                yacc[...] = part + bd_ref[...]
            else:
                yacc[...] += part
        yb[slot][...] = _pack_bf16_pairs(yacc[...])

    def drain(slot):
        nslot = 1 - slot
        src_copy(0, nslot).wait()
        dst_copy(0, nslot).wait()

        def body(r, carry):
            scatter_start(nslot, r)
            return carry
        lax.fori_loop(0, tm, body, 0, unroll=8)
        scatter_wait(slot)
        scatter_wait(nslot)
        gather_wait(slot)

    for parity in range(2):
        @pl.when((i < n_used) & (i % 2 == parity))
        def _():
            step(parity)

        @pl.when((i == n_used) & (i % 2 == parity))
        def _():
            drain(parity)


def _moe_gemm(xp, src_tab, dst_tab, block_expert, n_used, layer, wg, wu, bg, bu, wd, bd, tm):
    n, dp = xp.shape
    d = 2 * dp
    n_blocks = src_tab.shape[0]
    n_experts, hidden = wd.shape[1], wd.shape[2]
    hc = min(HC_MOE, hidden)
    assert hidden % hc == 0 and hc % LANES == 0 and tm % (hidden // hc) == 0
    assert dst_tab.shape[0] == n_blocks + 1 and block_expert.shape[0] == n_blocks + 1
    w_flat = lambda i, be, nu: (layer * n_experts + be[i], 0, 0)
    w_in = lambda i, be, nu: (be[i], 0, 0)
    grid_spec = pltpu.PrefetchScalarGridSpec(
        num_scalar_prefetch=2,
        grid=(n_blocks + 1,),
        in_specs=[
            pl.BlockSpec(memory_space=pl.ANY),
            pl.BlockSpec(memory_space=pl.ANY),
            pl.BlockSpec(memory_space=pl.ANY),
            pl.BlockSpec((None, d, hidden), w_flat),
            pl.BlockSpec((None, d, hidden), w_flat),
            pl.BlockSpec((None, 1, hidden), w_in),
            pl.BlockSpec((None, 1, hidden), w_in),
            pl.BlockSpec((None, None, hidden, d), lambda i, be, nu: (layer, be[i], 0, 0)),
            pl.BlockSpec((None, 1, d), w_in),
        ],
        out_specs=pl.BlockSpec(memory_space=pl.ANY),
        scratch_shapes=[
            pltpu.SMEM((2, tm), jnp.int32),
            pltpu.SMEM((2, tm), jnp.int32),
            pltpu.VMEM((tm, dp), jnp.uint32),
            pltpu.VMEM((tm, dp), jnp.uint32),
            pltpu.VMEM((tm, dp), jnp.uint32),
            pltpu.VMEM((tm, dp), jnp.uint32),
            pltpu.VMEM((tm, d), F32),
            pltpu.SemaphoreType.DMA((2,)),
            pltpu.SemaphoreType.DMA((2,)),
            pltpu.SemaphoreType.DMA((2,)),
            pltpu.SemaphoreType.DMA((2,)),
        ],
    )
    y_rows = n * TOP_K + 2 * tm
    return pl.pallas_call(
        functools.partial(_moe_gemm_kernel, tm=tm, hidden=hidden, hc=hc, n_blocks=n_blocks),
        grid_spec=grid_spec,
        out_shape=jax.ShapeDtypeStruct((y_rows, dp), jnp.uint32),
        compiler_params=_cparams(("arbitrary",)),
        name="moe_gemm",
    )(block_expert, n_used, src_tab, dst_tab, xp, wg, wu, bg, bu, wd, bd)


def _dispatch_tables(top_idx, tm, n_blocks):
    n = top_idx.shape[0]
    nk = n * TOP_K
    e_flat = top_idx.reshape(-1)
    onehot = (e_flat[:, None] == jnp.arange(N_EXPERTS, dtype=jnp.int32)[None, :]).astype(jnp.int32)
    csum = jnp.cumsum(onehot, axis=0)
    counts = csum[-1]
    rank = jnp.sum((csum - onehot) * onehot, axis=1)
    blocks_e = (counts + tm - 1) // tm
    bend = jnp.cumsum(blocks_e)
    bstart = bend - blocks_e
    dest = jnp.sum(onehot * bstart[None, :], axis=1) * tm + rank
    row_pair = jnp.full((n_blocks * tm,), -1, jnp.int32).at[dest].set(jnp.arange(nk, dtype=jnp.int32))
    row_pair = row_pair.reshape(n_blocks, tm)
    blk_ids = jnp.arange(n_blocks + 1, dtype=jnp.int32)
    trash = nk + (blk_ids[:n_blocks, None] % 2) * tm + jnp.arange(tm, dtype=jnp.int32)[None, :]
    src_tab = jnp.where(row_pair >= 0, row_pair // TOP_K, 0).astype(jnp.int32).reshape(n_blocks, 1, tm)
    dst_tab = jnp.where(row_pair >= 0, (row_pair % TOP_K) * n + row_pair // TOP_K, trash)
    before_first = nk + tm + jnp.arange(tm, dtype=jnp.int32)[None, :]
    dst_tab = jnp.concatenate([dst_tab, before_first], axis=0).astype(jnp.int32).reshape(n_blocks + 1, 1, tm)
    block_expert = jnp.minimum(jnp.sum((bend[None, :] <= blk_ids[:, None]).astype(jnp.int32), axis=1),
                               N_EXPERTS - 1).astype(jnp.int32)
    n_used = bend[-1:].astype(jnp.int32)
    return src_tab, dst_tab, block_expert, n_used


def _combine_kernel(*refs, alpha):
    y_refs, (x_ref, tg_ref, g_ref, b_ref, o_ref) = refs[:TOP_K], refs[TOP_K:]
    tg = tg_ref[...]
    y = None
    for k in range(TOP_K):
        yk = jnp.concatenate(_unpack_bf16_pairs(y_refs[k][...]), axis=1)
        y = tg[:, k:k + 1] * yk if y is None else y + tg[:, k:k + 1] * yk
    o_ref[...] = _layer_norm_rows(alpha * x_ref[...] + y, g_ref[...], b_ref[...])


def _combine_ln(y4, x1, gates, ln_g, ln_b, alpha):
    n, d = x1.shape
    tm = min(TM_COMBINE, n)
    assert n % tm == 0
    row = lambda i: (i, 0)
    fixed = lambda i: (0, 0)
    return pl.pallas_call(
        functools.partial(_combine_kernel, alpha=alpha),
        grid=(n // tm,),
        in_specs=[pl.BlockSpec((tm, d // 2), functools.partial(lambda k, i: (k * (n // tm) + i, 0), k))
                  for k in range(TOP_K)] + [
            pl.BlockSpec((tm, d), row),
            pl.BlockSpec((tm, LANES), row),
            pl.BlockSpec((1, d), fixed),
            pl.BlockSpec((1, d), fixed),
        ],
        out_specs=pl.BlockSpec((tm, d), row),
        out_shape=jax.ShapeDtypeStruct((n, d), F32),
        compiler_params=_cparams(("parallel",)),
        name="moe_combine_ln",
    )(*([y4] * TOP_K), x1, gates, ln_g.reshape(1, d).astype(F32), ln_b.reshape(1, d).astype(F32))


def _moe_layer(x1, xp, top_idx, gates, layer, wg, wu, b_gate_up, wd, b_down, ln_g, ln_b, alpha):
    n, d = x1.shape
    tm = min(TM_MOE, n)
    n_blocks = -(-(n * TOP_K) // tm) + N_EXPERTS
    src_tab, dst_tab, block_expert, n_used = _dispatch_tables(top_idx[:, :TOP_K], tm, n_blocks)
    hidden = wd.shape[2]
    bg = b_gate_up[:, 0::2].reshape(N_EXPERTS, 1, hidden).astype(F32)
    bu = b_gate_up[:, 1::2].reshape(N_EXPERTS, 1, hidden).astype(F32)
    bd = b_down.reshape(N_EXPERTS, 1, d).astype(F32)
    y4 = _moe_gemm(xp, src_tab, dst_tab, block_expert, n_used, layer, wg, wu, bg, bu, wd, bd, tm)
    return _combine_ln(y4, x1, gates, ln_g, ln_b, alpha)


def kernel(x, w_qkv_a, w_o_a, w_q_b, b_q_b, sinks_b, w_o_b, b_o_b, w_kv_shared, b_kv_shared, ln_mix_g, ln_mix_b, ln_ffn_g, ln_ffn_b, w_router, b_router, w_gate_up, b_gate_up, w_down, b_down):
    batch, seq, d = x.shape
    depth = ln_mix_g.shape[0]
    n_a = w_qkv_a.shape[0]
    alpha = (2 * depth) ** 0.25
    attn_w = N_HEADS * HEAD_DIM
    kv_a_w = N_KV_A * HEAD_DIM
    kv_b_w = N_KV_B * HEAD_DIM
    ga, gb = N_HEADS // N_KV_A, N_HEADS // N_KV_B
    rope = _rope_tables(seq)
    xs = x.reshape(batch * seq, d).astype(F32)
    kt_sh = v_sh = None
    wg_all, wu_all = _split_gate_up(w_gate_up)
    wd_all = w_down
    for layer in range(depth):
        if layer < n_a:
            q_pad, ka, va = _project(xs, w_qkv_a[layer].astype(BF16), jnp.zeros((attn_w + 2 * kv_a_w,), F32), rope,
                                     (attn_w, kv_a_w, kv_a_w), attn_w + kv_a_w, seq, tn=512,
                                     modes=('zero', 'block', 'one'))
            kmean = _kmean(ka, seq // MOBA_BLOCK)
            attn = _moba_attention(q_pad.reshape(batch, N_KV_A, ga, seq, LANES), ka.transpose(0, 1, 3, 2), va, kmean)
            attn2d = attn.reshape(batch * seq, attn_w)
            wo, bo = w_o_a[layer], jnp.zeros((d,), F32)
        else:
            jb = layer - n_a
            if jb == 0:
                k_pad, v_sh = _project(xs, w_kv_shared.astype(BF16), b_kv_shared, rope, (kv_b_w, kv_b_w), kv_b_w, seq,
                                       tn=256, modes=('zero', 'zero'))
                kt_sh = k_pad.transpose(0, 1, 3, 2)
            (q_pad,) = _project(xs, w_q_b[jb].astype(BF16), b_q_b[jb], rope, (attn_w,), attn_w, seq, tn=512,
                                modes=('zero',))
            attn = _swa_attention(q_pad.reshape(batch, N_KV_B, gb, seq, LANES), kt_sh, v_sh, sinks_b[jb])
            attn2d = attn.reshape(batch * seq, attn_w)
            wo, bo = w_o_b[jb], b_o_b[jb]
        x1, xp, top_idx, gates = _oproj_ln_router(attn2d, xs, wo.astype(BF16), bo, ln_mix_g[layer],
                                                  ln_mix_b[layer], w_router[layer], b_router[layer], alpha)
        xs = _moe_layer(x1, xp, top_idx, gates, layer, wg_all, wu_all, b_gate_up[layer], wd_all, b_down[layer],
                        ln_ffn_g[layer], ln_ffn_b[layer], alpha)
    return xs.reshape(batch, seq, d).astype(x.dtype)
```

```python
import functools

import numpy as np
import jax
import jax.numpy as jnp
from jax import lax
from jax.experimental import pallas as pl
from jax.experimental.pallas import tpu as pltpu

F32 = jnp.float32
BF16 = jnp.bfloat16
NEG_INF = float("-inf")

HEAD_DIM = 64
N_HEADS = 32
N_KV_A = 8
N_KV_B = 4
ROPE_DIM = 16
ROPE_THETA = 500000.0
MOBA_BLOCK = 256
MOBA_TOPK = 3
SWA_WINDOW = 128
N_EXPERTS = 32
TOP_K = 4
SWIGLU_LIMIT = 7.0
SWIGLU_ALPHA = 1.702
LN_EPS = 1e-5

MOBA_MASK_BIAS = -(2.0 ** 100)

LANES = 128
SUBLANES = 8
V7X_VMEM_BYTES = 64 * 1024 * 1024
VMEM_LIMIT = V7X_VMEM_BYTES - 8 * 1024 * 1024

MOBA_KV_TILE = 512
TM_PROJ = 512
TM_OPROJ = 512
TM_MOE = 512
TM_COMBINE = 256
HC_MOE = 256
TR_SPLIT = 1024
ROW_DMA_PRIORITY = 1


def _cparams(semantics):
    return pltpu.CompilerParams(dimension_semantics=semantics, vmem_limit_bytes=VMEM_LIMIT)


def _proj_kernel(x_ref, w_ref, b_ref, c_ref, sa_ref, sb_ref, *o_refs, splits, modes, rope_cols, tn, tm, pos_blocks):
    x = x_ref[...].astype(BF16)
    lane = lax.broadcasted_iota(jnp.int32, (tm, LANES), 1)
    lower = lane < HEAD_DIM
    fills = {}
    if 'zero' in modes:
        fills['zero'] = jnp.zeros((tm, LANES), F32)
    if 'one' in modes:
        fills['one'] = jnp.where(lane == HEAD_DIM, 1.0, 0.0)
    if 'block' in modes:
        pos = (pl.program_id(0) % pos_blocks) * tm + lax.broadcasted_iota(jnp.int32, (tm, LANES), 0)
        fills['block'] = jnp.where(lane - HEAD_DIM == pos // MOBA_BLOCK, 1.0, 0.0)
    col = 0
    for o_ref, width, mode in zip(o_refs, splits, modes):
        for c0 in range(0, width, tn):
            lo = col + c0
            y = jnp.dot(x, w_ref[:, lo:lo + tn], preferred_element_type=F32) + b_ref[:, lo:lo + tn]
            for l0 in range(0, tn, LANES):
                seg = y[:, l0:l0 + LANES]
                if lo < rope_cols:
                    seg = (seg * c_ref[...] + pltpu.roll(seg, LANES - ROPE_DIM // 2, 1) * sa_ref[...]
                           + pltpu.roll(seg, ROPE_DIM // 2, 1) * sb_ref[...])
                if mode is None:
                    o_ref[:, c0 + l0:c0 + l0 + LANES] = seg.astype(o_ref.dtype)
                else:
                    head = (c0 + l0) // HEAD_DIM
                    o_ref[head] = jnp.where(lower, seg, fills[mode]).astype(o_ref.dtype)
                    o_ref[head + 1] = jnp.where(lower, pltpu.roll(seg, HEAD_DIM, 1), fills[mode]).astype(o_ref.dtype)
        col += width


def _project(x2d, w_bf16, bias, rope_tabs, splits, rope_cols, seq, tn, modes=None):
    n, d = x2d.shape
    nout = w_bf16.shape[1]
    tm = min(TM_PROJ, seq)
    modes = tuple(modes) if modes is not None else (None,) * len(splits)
    assert n % tm == 0 and seq % tm == 0 and sum(splits) == nout
    assert all(s % tn == 0 for s in splits) and rope_cols % tn == 0 and tn % LANES == 0
    assert 2 * HEAD_DIM == LANES and (tm % MOBA_BLOCK == 0 or 'block' not in modes)
    pos_blocks = seq // tm
    batch = n // seq
    c_tab, sa_tab, sb_tab = rope_tabs
    tab_spec = pl.BlockSpec((tm, LANES), lambda i: (i % pos_blocks, 0))
    out_specs, out_shape = [], []
    for s, mode in zip(splits, modes):
        if mode is None:
            out_specs.append(pl.BlockSpec((tm, s), lambda i: (i, 0)))
            out_shape.append(jax.ShapeDtypeStruct((n, s), BF16))
        else:
            heads = s // HEAD_DIM
            out_specs.append(pl.BlockSpec((None, heads, tm, LANES), lambda i: (i // pos_blocks, 0, i % pos_blocks, 0)))
            out_shape.append(jax.ShapeDtypeStruct((batch, heads, seq, LANES), BF16))
    kern = functools.partial(_proj_kernel, splits=tuple(splits), modes=modes, rope_cols=rope_cols, tn=tn, tm=tm,
                             pos_blocks=pos_blocks)
    return pl.pallas_call(
        kern,
        grid=(n // tm,),
        in_specs=[
            pl.BlockSpec((tm, d), lambda i: (i, 0)),
            pl.BlockSpec((d, nout), lambda i: (0, 0)),
            pl.BlockSpec((1, nout), lambda i: (0, 0)),
            tab_spec, tab_spec, tab_spec,
        ],
        out_specs=out_specs,
        out_shape=out_shape,
        compiler_params=_cparams(("parallel",)),
        name="proj_rope",
    )(x2d, w_bf16, bias.reshape(1, nout).astype(F32), c_tab, sa_tab, sb_tab)


def _rope_tables(seq):
    half = ROPE_DIM // 2
    inv = ROPE_THETA ** (-jnp.arange(0, ROPE_DIM, 2, dtype=F32) / ROPE_DIM)
    ang = jnp.arange(seq, dtype=F32)[:, None] * inv[None, :]
    cos, sin = jnp.cos(ang), jnp.sin(ang)
    ones = jnp.ones((seq, HEAD_DIM - ROPE_DIM), F32)
    zeros = jnp.zeros((seq, HEAD_DIM - ROPE_DIM), F32)
    zh = jnp.zeros((seq, half), F32)
    c_head = jnp.concatenate([cos, cos, ones], axis=1)
    sa_head = jnp.concatenate([-sin, zh, zeros], axis=1)
    sb_head = jnp.concatenate([zh, sin, zeros], axis=1)
    rep = LANES // HEAD_DIM
    return tuple(jnp.tile(t, (1, rep)) for t in (c_head, sa_head, sb_head))


def _kmean_kernel(k_ref, o_ref, *, nb, blk):
    k = k_ref[...].astype(F32)
    km = jnp.mean(k.reshape(nb, blk, LANES), axis=1)
    lane = lax.broadcasted_iota(jnp.int32, (nb, LANES), 1)
    o_ref[...] = jnp.zeros(o_ref.shape, F32)
    o_ref[0:nb, :] = jnp.where(lane < HEAD_DIM, km, 0.0)


def _kmean(ka, nb):
    b, hkv, s, _ = ka.shape
    assert nb <= LANES
    return pl.pallas_call(
        functools.partial(_kmean_kernel, nb=nb, blk=MOBA_BLOCK),
        grid=(b, hkv),
        in_specs=[pl.BlockSpec((None, None, s, LANES), lambda i, j: (i, j, 0, 0))],
        out_specs=pl.BlockSpec((None, None, LANES, LANES), lambda i, j: (i, j, 0, 0)),
        out_shape=jax.ShapeDtypeStruct((b, hkv, LANES, LANES), F32),
        compiler_params=_cparams(("parallel", "parallel")),
        name="moba_kmean",
    )(ka)


def _moba_kernel(it_ref, jt_ref, q_ref, kta_ref, va_ref, km_ref, o_ref, qa_scr, m_scr, acc_scr,
                 *, groups, blk, kv_tile, nbp, scale):
    p = pl.program_id(2)
    i = it_ref[p]
    jt = jt_ref[p]
    own_tile = i // (kv_tile // blk)
    is_diag = jt == own_tile
    rows = groups * blk

    def softmax_step(s, first):
        parts = [s[:, c:c + LANES] for c in range(0, kv_tile, LANES)]
        mx = parts[0]
        for part in parts[1:]:
            mx = jnp.maximum(mx, part)
        rm = jnp.max(mx, axis=-1, keepdims=True)
        if first:
            m_new = jnp.broadcast_to(rm, (rows, LANES))
        else:
            m_old = m_scr[...]
            m_new = jnp.maximum(m_old, rm)
        pr = jnp.concatenate([jnp.exp(part - m_new) for part in parts], axis=1).astype(BF16)
        pv = jnp.dot(pr, va_ref[...], preferred_element_type=F32)
        if first:
            acc_scr[...] = pv
        else:
            acc_scr[...] = jnp.exp(m_old - m_new) * acc_scr[...] + pv
        m_scr[...] = m_new

    @pl.when(is_diag)
    def _():
        qf = q_ref[...].reshape(rows, LANES).astype(F32)
        gate = lax.dot_general(km_ref[0:nbp, :], qf, (((1,), (1,)), ((), ())),
                               precision=lax.Precision.HIGHEST, preferred_element_type=F32)
        kb = lax.broadcasted_iota(jnp.int32, (nbp, rows), 0)
        kb_f = kb.astype(F32)
        cur = jnp.where(kb < i, gate, NEG_INF)
        sel = jnp.zeros((nbp, rows), F32)
        for _ in range(MOBA_TOPK):
            mx = jnp.max(cur, axis=0, keepdims=True)
            idx = jnp.min(jnp.where(cur == mx, kb_f, float(nbp)), axis=0, keepdims=True)
            hit = kb_f == idx
            sel = jnp.where(hit & (mx > NEG_INF), 1.0, sel)
            cur = jnp.where(hit, NEG_INF, cur)
        bias_t = jnp.where((sel > 0.0) | (kb >= i), 0.0, MOBA_MASK_BIAS)
        bias_t = jnp.concatenate([jnp.zeros((HEAD_DIM, rows), F32), bias_t,
                                  jnp.zeros((LANES - HEAD_DIM - nbp, rows), F32)], axis=0)
        qa = (qf * scale + bias_t.T).astype(BF16)
        qa_scr[...] = qa
        s = jnp.dot(qa, kta_ref[...], preferred_element_type=F32)
        qpos = i * blk + lax.broadcasted_iota(jnp.int32, (groups, blk, kv_tile), 1).reshape(rows, kv_tile)
        kpos = jt * kv_tile + lax.broadcasted_iota(jnp.int32, (rows, kv_tile), 1)
        s = jnp.where(kpos <= qpos, s, NEG_INF)
        softmax_step(s, True)

    @pl.when(jnp.logical_not(is_diag))
    def _():
        s = jnp.dot(qa_scr[...], kta_ref[...], preferred_element_type=F32)
        softmax_step(s, False)

    @pl.when((jt == own_tile - 1) | (own_tile == 0))
    def _():
        acc = acc_scr[...]
        o = acc / acc[:, HEAD_DIM:HEAD_DIM + 1]
        lower = lax.broadcasted_iota(jnp.int32, (blk, LANES), 1) < HEAD_DIM
        pairs = [jnp.where(lower, o[g * blk:(g + 1) * blk], pltpu.roll(o[(g + 1) * blk:(g + 2) * blk], HEAD_DIM, 1))
                 for g in range(0, groups, 2)]
        o_ref[...] = jnp.concatenate(pairs, axis=1).astype(o_ref.dtype)


def _moba_attention(q_pad, kta, va, kmean):
    b, hkv, groups, s, _ = q_pad.shape
    blk = MOBA_BLOCK
    assert s % blk == 0 and blk == 2 * LANES
    nb = s // blk
    assert nb <= LANES - HEAD_DIM
    kv_tile = MOBA_KV_TILE if s % MOBA_KV_TILE == 0 else blk
    per_tile = kv_tile // blk
    it, jt = [], []
    for i in range(nb):
        it.append(i)
        jt.append(i // per_tile)
        for j in range(i // per_tile):
            it.append(i)
            jt.append(j)
    it = jnp.asarray(np.asarray(it, np.int32))
    jt = jnp.asarray(np.asarray(jt, np.int32))
    nbp = -(-nb // SUBLANES) * SUBLANES
    assert nbp < LANES - HEAD_DIM
    kern = functools.partial(_moba_kernel, groups=groups, blk=blk, kv_tile=kv_tile, nbp=nbp,
                             scale=HEAD_DIM ** -0.5)
    grid_spec = pltpu.PrefetchScalarGridSpec(
        num_scalar_prefetch=2,
        grid=(b, hkv, int(it.shape[0])),
        in_specs=[
            pl.BlockSpec((None, None, groups, blk, LANES), lambda bi, h, p, it, jt: (bi, h, 0, it[p], 0)),
            pl.BlockSpec((None, None, LANES, kv_tile), lambda bi, h, p, it, jt: (bi, h, 0, jt[p])),
            pl.BlockSpec((None, None, kv_tile, LANES), lambda bi, h, p, it, jt: (bi, h, jt[p], 0)),
            pl.BlockSpec((None, None, LANES, LANES), lambda bi, h, p, it, jt: (bi, h, 0, 0)),
        ],
        out_specs=pl.BlockSpec((None, blk, groups * HEAD_DIM), lambda bi, h, p, it, jt: (bi, it[p], h)),
        scratch_shapes=[
            pltpu.VMEM((groups * blk, LANES), BF16),
            pltpu.VMEM((groups * blk, LANES), F32),
            pltpu.VMEM((groups * blk, LANES), F32),
        ],
    )
    return pl.pallas_call(
        kern,
        grid_spec=grid_spec,
        out_shape=jax.ShapeDtypeStruct((b, s, hkv * groups * HEAD_DIM), BF16),
        compiler_params=_cparams(("parallel", "parallel", "arbitrary")),
        name="moba_attn",
    )(it, jt, q_pad, kta, va, kmean)


def _swa_kernel(q_ref, ktp_ref, ktc_ref, vp_ref, vc_ref, sink_ref, o_ref, *, groups, win, scale):
    kh = pl.program_id(1)
    i = pl.program_id(2)
    row = lax.broadcasted_iota(jnp.int32, (win, win), 0)
    col = lax.broadcasted_iota(jnp.int32, (win, win), 1)
    cur_mask = col <= row
    prev_mask = (col > row) & (i > 0)
    outs = []
    for g in range(groups):
        qg = q_ref[g]
        sp = jnp.dot(qg, ktp_ref[...], preferred_element_type=F32) * scale
        sc = jnp.dot(qg, ktc_ref[...], preferred_element_type=F32) * scale
        sp = jnp.where(prev_mask, sp, NEG_INF)
        sc = jnp.where(cur_mask, sc, NEG_INF)
        sink = sink_ref[kh * groups + g]
        mx = jnp.maximum(jnp.maximum(jnp.max(sp, axis=-1, keepdims=True), jnp.max(sc, axis=-1, keepdims=True)), sink)
        pp = jnp.exp(sp - mx)
        pc = jnp.exp(sc - mx)
        den = jnp.sum(pp, axis=-1, keepdims=True) + jnp.sum(pc, axis=-1, keepdims=True) + jnp.exp(sink - mx)
        o = (jnp.dot(pp.astype(BF16), vp_ref[...], preferred_element_type=F32)
             + jnp.dot(pc.astype(BF16), vc_ref[...], preferred_element_type=F32))
        outs.append(o / den)
    lower = lax.broadcasted_iota(jnp.int32, (win, LANES), 1) < HEAD_DIM
    pairs = [jnp.where(lower, outs[g], pltpu.roll(outs[g + 1], HEAD_DIM, 1)) for g in range(0, groups, 2)]
    o_ref[...] = jnp.concatenate(pairs, axis=1).astype(o_ref.dtype)


def _swa_attention(q_pad, kt_pad, v_pad, sinks):
    b, hkv, groups, s, _ = q_pad.shape
    win = SWA_WINDOW
    assert s % win == 0 and groups % 2 == 0
    nb = s // win
    kern = functools.partial(_swa_kernel, groups=groups, win=win, scale=HEAD_DIM ** -0.5)
    return pl.pallas_call(
        kern,
        grid=(b, hkv, nb),
        in_specs=[
            pl.BlockSpec((None, None, groups, win, LANES), lambda bi, h, i: (bi, h, 0, i, 0)),
            pl.BlockSpec((None, None, LANES, win), lambda bi, h, i: (bi, h, 0, jnp.maximum(i - 1, 0))),
            pl.BlockSpec((None, None, LANES, win), lambda bi, h, i: (bi, h, 0, i)),
            pl.BlockSpec((None, None, win, LANES), lambda bi, h, i: (bi, h, jnp.maximum(i - 1, 0), 0)),
            pl.BlockSpec((None, None, win, LANES), lambda bi, h, i: (bi, h, i, 0)),
            pl.BlockSpec(memory_space=pltpu.SMEM),
        ],
        out_specs=pl.BlockSpec((None, win, groups * HEAD_DIM), lambda bi, h, i: (bi, i, h)),
        out_shape=jax.ShapeDtypeStruct((b, s, hkv * groups * HEAD_DIM), BF16),
        compiler_params=_cparams(("parallel", "parallel", "parallel")),
        name="swa_attn",
    )(q_pad, kt_pad, kt_pad, v_pad, v_pad, sinks.astype(F32))


def _layer_norm_rows(z, g, b):
    mu = jnp.mean(z, axis=-1, keepdims=True)
    zc = z - mu
    var = jnp.mean(zc * zc, axis=-1, keepdims=True)
    return zc * lax.rsqrt(var + LN_EPS) * g + b


def _pack_bf16_pairs(v):
    half = v.shape[1] // 2
    hi = lax.bitcast_convert_type(v[:, :half].astype(BF16).astype(F32), jnp.uint32)
    lo = lax.bitcast_convert_type(v[:, half:].astype(BF16).astype(F32), jnp.uint32)
    return hi | (lo >> 16)


def _unpack_bf16_pairs(u):
    hi = lax.bitcast_convert_type(u & jnp.uint32(0xFFFF0000), F32)
    lo = lax.bitcast_convert_type(u << 16, F32)
    return hi, lo


def _oproj_kernel(a_ref, x_ref, wo_ref, bo_ref, g_ref, b_ref, wrh_ref, wrl_ref, br_ref, x1_ref, xp_ref, ti_ref, tg_ref,
                  *, alpha, n_experts):
    mix = jnp.dot(a_ref[...], wo_ref[...], preferred_element_type=F32) + bo_ref[...]
    x1 = _layer_norm_rows(alpha * x_ref[...] + mix, g_ref[...], b_ref[...])
    x1_ref[...] = x1
    xp_ref[...] = _pack_bf16_pairs(x1)
    xh = x1.astype(BF16)
    xl = (x1 - xh.astype(F32)).astype(BF16)
    logits = (jnp.dot(xh, wrh_ref[...], preferred_element_type=F32)
              + jnp.dot(xh, wrl_ref[...], preferred_element_type=F32)
              + jnp.dot(xl, wrh_ref[...], preferred_element_type=F32)) + br_ref[...]
    lane = lax.broadcasted_iota(jnp.int32, logits.shape, 1)
    cur = jnp.where(lane < n_experts, logits, NEG_INF)
    vals, idxs = [], []
    for _ in range(TOP_K):
        mx = jnp.max(cur, axis=-1, keepdims=True)
        idx = jnp.min(jnp.where(cur == mx, lane, LANES), axis=-1, keepdims=True)
        vals.append(mx)
        idxs.append(idx)
        cur = jnp.where(lane == idx, NEG_INF, cur)
    exps = [jnp.exp(v - vals[0]) for v in vals]
    den = exps[0]
    for e in exps[1:]:
        den = den + e
    ti = jnp.zeros(logits.shape, jnp.int32)
    tg = jnp.zeros(logits.shape, F32)
    for k in range(TOP_K):
        ti = jnp.where(lane == k, idxs[k], ti)
        tg = jnp.where(lane == k, exps[k] / den, tg)
    ti_ref[...] = ti
    tg_ref[...] = tg


def _oproj_ln_router(attn2d, x2d, wo_bf16, bo, ln_g, ln_b, w_r, b_r, alpha):
    n, d = x2d.shape
    da = attn2d.shape[1]
    tm = min(TM_OPROJ, n)
    assert n % tm == 0
    n_experts = w_r.shape[1]
    wr_pad = jnp.zeros((d, LANES), F32).at[:, :n_experts].set(w_r.astype(F32))
    wr_hi = wr_pad.astype(BF16)
    wr_lo = (wr_pad - wr_hi.astype(F32)).astype(BF16)
    br_pad = jnp.zeros((1, LANES), F32).at[0, :n_experts].set(b_r.astype(F32))
    row = lambda i: (i, 0)
    fixed = lambda i: (0, 0)
    return pl.pallas_call(
        functools.partial(_oproj_kernel, alpha=alpha, n_experts=n_experts),
        grid=(n // tm,),
        in_specs=[
            pl.BlockSpec((tm, da), row),
            pl.BlockSpec((tm, d), row),
            pl.BlockSpec((da, d), fixed),
            pl.BlockSpec((1, d), fixed),
            pl.BlockSpec((1, d), fixed),
            pl.BlockSpec((1, d), fixed),
            pl.BlockSpec((d, LANES), fixed),
            pl.BlockSpec((d, LANES), fixed),
            pl.BlockSpec((1, LANES), fixed),
        ],
        out_specs=[pl.BlockSpec((tm, d), row), pl.BlockSpec((tm, d // 2), row), pl.BlockSpec((tm, LANES), row),
                   pl.BlockSpec((tm, LANES), row)],
        out_shape=[jax.ShapeDtypeStruct((n, d), F32), jax.ShapeDtypeStruct((n, d // 2), jnp.uint32),
                   jax.ShapeDtypeStruct((n, LANES), jnp.int32), jax.ShapeDtypeStruct((n, LANES), F32)],
        compiler_params=_cparams(("parallel",)),
        name="oproj_ln_router",
    )(attn2d, x2d, wo_bf16, bo.reshape(1, d).astype(F32), ln_g.reshape(1, d).astype(F32),
      ln_b.reshape(1, d).astype(F32), wr_hi, wr_lo, br_pad)


def _split_gate_up_kernel(w_ref, pg_ref, pu_ref, g_ref, u_ref, *, width):
    for c in range(0, width, 2 * LANES):
        blk = w_ref[:, c:c + 2 * LANES].astype(BF16)
        g_ref[:, c // 2:c // 2 + LANES] = jnp.dot(blk, pg_ref[...], preferred_element_type=F32).astype(BF16)
        u_ref[:, c // 2:c // 2 + LANES] = jnp.dot(blk, pu_ref[...], preferred_element_type=F32).astype(BF16)


def _split_gate_up(w_gate_up):
    nl, ne, d, width = w_gate_up.shape
    assert width % (2 * LANES) == 0
    tr = min(TR_SPLIT, d)
    assert d % tr == 0
    k = jnp.arange(2 * LANES, dtype=jnp.int32)[:, None]
    j = jnp.arange(LANES, dtype=jnp.int32)[None, :]
    pick_gate = (k == 2 * j).astype(BF16)
    pick_lin = (k == 2 * j + 1).astype(BF16)
    out = jax.ShapeDtypeStruct((nl * ne, d, width // 2), BF16)
    return pl.pallas_call(
        functools.partial(_split_gate_up_kernel, width=width),
        grid=(nl * ne, d // tr),
        in_specs=[
            pl.BlockSpec((None, tr, width), lambda e, r: (e, r, 0)),
            pl.BlockSpec((2 * LANES, LANES), lambda e, r: (0, 0)),
            pl.BlockSpec((2 * LANES, LANES), lambda e, r: (0, 0)),
        ],
        out_specs=[pl.BlockSpec((None, tr, width // 2), lambda e, r: (e, r, 0))] * 2,
        out_shape=[out, out],
        compiler_params=_cparams(("parallel", "parallel")),
        name="split_gate_up",
    )(w_gate_up.reshape(nl * ne, d, width), pick_gate, pick_lin)


def _moe_gemm_kernel(be_ref, nu_ref, src_hbm, dst_hbm, x_hbm, wg_ref, wu_ref, bg_ref, bu_ref, wd_ref, bd_ref, y_hbm,
                     gtab, stab, xg0, xg1, yb0, yb1, yacc, gtsem, stsem, gsem, ssem, *, tm, hidden, hc, n_blocks):
    del be_ref
    i = pl.program_id(0)
    n_used = nu_ref[0]
    last = n_used - 1
    xg = (xg0, xg1)
    yb = (yb0, yb1)

    def src_copy(blk, s):
        return pltpu.make_async_copy(src_hbm.at[blk], gtab.at[pl.ds(s, 1)], gtsem.at[s])

    def dst_copy(blk, s):
        return pltpu.make_async_copy(dst_hbm.at[blk], stab.at[pl.ds(s, 1)], stsem.at[s])

    def gather_start(s, r):
        tok = gtab[s, r]
        pltpu.make_async_copy(x_hbm.at[pl.ds(tok, 1)], xg[s].at[pl.ds(r, 1)], gsem.at[s]).start(ROW_DMA_PRIORITY)

    def gather_wait(s):
        pltpu.make_async_copy(x_hbm.at[pl.ds(0, tm)], xg[s], gsem.at[s]).wait()

    def scatter_start(s, r):
        row = stab[s, r]
        pltpu.make_async_copy(yb[s].at[pl.ds(r, 1)], y_hbm.at[pl.ds(row, 1)], ssem.at[s]).start(ROW_DMA_PRIORITY)

    def scatter_wait(s):
        pltpu.make_async_copy(yb[s], y_hbm.at[pl.ds(0, tm)], ssem.at[s]).wait()

    @pl.when(i == 0)
    def _():
        src_copy(0, 0).start()
        src_copy(jnp.minimum(1, last), 1).start()
        dst_copy(n_blocks, 1).start()
        src_copy(0, 0).wait()

        def body(r, carry):
            gather_start(0, r)
            return carry
        lax.fori_loop(0, tm, body, 0, unroll=8)
        yb1[...] = jnp.zeros(yb1.shape, yb1.dtype)
        tail = y_hbm.shape[0] - 2 * tm
        for half in range(2):
            pltpu.make_async_copy(yb1, y_hbm.at[pl.ds(tail + half * tm, tm)], ssem.at[1]).start()
        for half in range(2):
            pltpu.make_async_copy(yb1, y_hbm.at[pl.ds(tail + half * tm, tm)], ssem.at[1]).wait()

    def step(slot):
        nslot = 1 - slot
        src_copy(jnp.minimum(i + 2, last), slot).start()
        dst_copy(i, slot).start()
        src_copy(0, nslot).wait()
        dst_copy(0, nslot).wait()
        gather_wait(slot)

        @pl.when(i >= 1)
        def _():
            scatter_wait(slot)

        x_hi, x_lo = _unpack_bf16_pairs(xg[slot][...])
        xb = jnp.concatenate([x_hi.astype(BF16), x_lo.astype(BF16)], axis=1)
        n_chunks = hidden // hc
        rows_per_chunk = tm // n_chunks
        for ci in range(n_chunks):
            c = ci * hc
            r0 = ci * rows_per_chunk

            @pl.when(n_used > 0)
            def _():
                for r in range(r0, r0 + rows_per_chunk):
                    gather_start(nslot, r)

            glu = jnp.dot(xb, wg_ref[:, c:c + hc], preferred_element_type=F32) + bg_ref[:, c:c + hc]
            lin = jnp.dot(xb, wu_ref[:, c:c + hc], preferred_element_type=F32) + bu_ref[:, c:c + hc]
            glu = jnp.minimum(glu, SWIGLU_LIMIT)
            lin = jnp.clip(lin, -SWIGLU_LIMIT, SWIGLU_LIMIT)
            act = glu * jax.nn.sigmoid(SWIGLU_ALPHA * glu) * (lin + 1.0)

            @pl.when(n_used > 0)
            def _():
                for r in range(r0, r0 + rows_per_chunk):
                    scatter_start(nslot, r)

            part = jnp.dot(act.astype(BF16), wd_ref[c:c + hc, :].astype(BF16), preferred_element_type=F32)
            if ci == 0:
                yacc[...] = part + bd_ref[...]
            else:
                yacc[...] += part
        yb[slot][...] = _pack_bf16_pairs(yacc[...])

    def drain(slot):
        nslot = 1 - slot
        src_copy(0, nslot).wait()
        dst_copy(0, nslot).wait()

        def body(r, carry):
            scatter_start(nslot, r)
            return carry
        lax.fori_loop(0, tm, body, 0, unroll=8)
        scatter_wait(slot)
        scatter_wait(nslot)
        gather_wait(slot)

    for parity in range(2):
        @pl.when((i < n_used) & (i % 2 == parity))
        def _():
            step(parity)

        @pl.when((i == n_used) & (i % 2 == parity))
        def _():
            drain(parity)


def _moe_gemm(xp, src_tab, dst_tab, block_expert, n_used, layer, wg, wu, bg, bu, wd, bd, tm):
    n, dp = xp.shape
    d = 2 * dp
    n_blocks = src_tab.shape[0]
    n_experts, hidden = wd.shape[1], wd.shape[2]
    hc = min(HC_MOE, hidden)
    assert hidden % hc == 0 and hc % LANES == 0 and tm % (hidden // hc) == 0
    assert dst_tab.shape[0] == n_blocks + 1 and block_expert.shape[0] == n_blocks + 1
    w_flat = lambda i, be, nu: (layer * n_experts + be[i], 0, 0)
    w_in = lambda i, be, nu: (be[i], 0, 0)
    grid_spec = pltpu.PrefetchScalarGridSpec(
        num_scalar_prefetch=2,
        grid=(n_blocks + 1,),
        in_specs=[
            pl.BlockSpec(memory_space=pl.ANY),
            pl.BlockSpec(memory_space=pl.ANY),
            pl.BlockSpec(memory_space=pl.ANY),
            pl.BlockSpec((None, d, hidden), w_flat),
            pl.BlockSpec((None, d, hidden), w_flat),
            pl.BlockSpec((None, 1, hidden), w_in),
            pl.BlockSpec((None, 1, hidden), w_in),
            pl.BlockSpec((None, None, hidden, d), lambda i, be, nu: (layer, be[i], 0, 0)),
            pl.BlockSpec((None, 1, d), w_in),
        ],
        out_specs=pl.BlockSpec(memory_space=pl.ANY),
        scratch_shapes=[
            pltpu.SMEM((2, tm), jnp.int32),
            pltpu.SMEM((2, tm), jnp.int32),
            pltpu.VMEM((tm, dp), jnp.uint32),
            pltpu.VMEM((tm, dp), jnp.uint32),
            pltpu.VMEM((tm, dp), jnp.uint32),
            pltpu.VMEM((tm, dp), jnp.uint32),
            pltpu.VMEM((tm, d), F32),
            pltpu.SemaphoreType.DMA((2,)),
            pltpu.SemaphoreType.DMA((2,)),
            pltpu.SemaphoreType.DMA((2,)),
            pltpu.SemaphoreType.DMA((2,)),
        ],
    )
    y_rows = n * TOP_K + 2 * tm
    return pl.pallas_call(
        functools.partial(_moe_gemm_kernel, tm=tm, hidden=hidden, hc=hc, n_blocks=n_blocks),
        grid_spec=grid_spec,
        out_shape=jax.ShapeDtypeStruct((y_rows, dp), jnp.uint32),
        compiler_params=_cparams(("arbitrary",)),
        name="moe_gemm",
    )(block_expert, n_used, src_tab, dst_tab, xp, wg, wu, bg, bu, wd, bd)


def _dispatch_tables(top_idx, tm, n_blocks):
    n = top_idx.shape[0]
    nk = n * TOP_K
    e_flat = top_idx.reshape(-1)
    onehot = (e_flat[:, None] == jnp.arange(N_EXPERTS, dtype=jnp.int32)[None, :]).astype(jnp.int32)
    csum = jnp.cumsum(onehot, axis=0)
    counts = csum[-1]
    rank = jnp.sum((csum - onehot) * onehot, axis=1)
    blocks_e = (counts + tm - 1) // tm
    bend = jnp.cumsum(blocks_e)
    bstart = bend - blocks_e
    dest = jnp.sum(onehot * bstart[None, :], axis=1) * tm + rank
    row_pair = jnp.full((n_blocks * tm,), -1, jnp.int32).at[dest].set(jnp.arange(nk, dtype=jnp.int32))
    row_pair = row_pair.reshape(n_blocks, tm)
    blk_ids = jnp.arange(n_blocks + 1, dtype=jnp.int32)
    trash = nk + (blk_ids[:n_blocks, None] % 2) * tm + jnp.arange(tm, dtype=jnp.int32)[None, :]
    src_tab = jnp.where(row_pair >= 0, row_pair // TOP_K, 0).astype(jnp.int32).reshape(n_blocks, 1, tm)
    dst_tab = jnp.where(row_pair >= 0, (row_pair % TOP_K) * n + row_pair // TOP_K, trash)
    before_first = nk + tm + jnp.arange(tm, dtype=jnp.int32)[None, :]
    dst_tab = jnp.concatenate([dst_tab, before_first], axis=0).astype(jnp.int32).reshape(n_blocks + 1, 1, tm)
    block_expert = jnp.minimum(jnp.sum((bend[None, :] <= blk_ids[:, None]).astype(jnp.int32), axis=1),
                               N_EXPERTS - 1).astype(jnp.int32)
    n_used = bend[-1:].astype(jnp.int32)
    return src_tab, dst_tab, block_expert, n_used


def _combine_kernel(*refs, alpha):
    y_refs, (x_ref, tg_ref, g_ref, b_ref, o_ref) = refs[:TOP_K], refs[TOP_K:]
    tg = tg_ref[...]
    y = None
    for k in range(TOP_K):
        yk = jnp.concatenate(_unpack_bf16_pairs(y_refs[k][...]), axis=1)
        y = tg[:, k:k + 1] * yk if y is None else y + tg[:, k:k + 1] * yk
    o_ref[...] = _layer_norm_rows(alpha * x_ref[...] + y, g_ref[...], b_ref[...])


def _combine_ln(y4, x1, gates, ln_g, ln_b, alpha):
    n, d = x1.shape
    tm = min(TM_COMBINE, n)
    assert n % tm == 0
    row = lambda i: (i, 0)
    fixed = lambda i: (0, 0)
    return pl.pallas_call(
        functools.partial(_combine_kernel, alpha=alpha),
        grid=(n // tm,),
        in_specs=[pl.BlockSpec((tm, d // 2), functools.partial(lambda k, i: (k * (n // tm) + i, 0), k))
                  for k in range(TOP_K)] + [
            pl.BlockSpec((tm, d), row),
            pl.BlockSpec((tm, LANES), row),
            pl.BlockSpec((1, d), fixed),
            pl.BlockSpec((1, d), fixed),
        ],
        out_specs=pl.BlockSpec((tm, d), row),
        out_shape=jax.ShapeDtypeStruct((n, d), F32),
        compiler_params=_cparams(("parallel",)),
        name="moe_combine_ln",
    )(*([y4] * TOP_K), x1, gates, ln_g.reshape(1, d).astype(F32), ln_b.reshape(1, d).astype(F32))


def _moe_layer(x1, xp, top_idx, gates, layer, wg, wu, b_gate_up, wd, b_down, ln_g, ln_b, alpha):
    n, d = x1.shape
    tm = min(TM_MOE, n)
    n_blocks = -(-(n * TOP_K) // tm) + N_EXPERTS
    src_tab, dst_tab, block_expert, n_used = _dispatch_tables(top_idx[:, :TOP_K], tm, n_blocks)
    hidden = wd.shape[2]
    bg = b_gate_up[:, 0::2].reshape(N_EXPERTS, 1, hidden).astype(F32)
    bu = b_gate_up[:, 1::2].reshape(N_EXPERTS, 1, hidden).astype(F32)
    bd = b_down.reshape(N_EXPERTS, 1, d).astype(F32)
    y4 = _moe_gemm(xp, src_tab, dst_tab, block_expert, n_used, layer, wg, wu, bg, bu, wd, bd, tm)
    return _combine_ln(y4, x1, gates, ln_g, ln_b, alpha)


def kernel(x, w_qkv_a, w_o_a, w_q_b, b_q_b, sinks_b, w_o_b, b_o_b, w_kv_shared, b_kv_shared, ln_mix_g, ln_mix_b, ln_ffn_g, ln_ffn_b, w_router, b_router, w_gate_up, b_gate_up, w_down, b_down):
    batch, seq, d = x.shape
    depth = ln_mix_g.shape[0]
    n_a = w_qkv_a.shape[0]
    alpha = (2 * depth) ** 0.25
    attn_w = N_HEADS * HEAD_DIM
    kv_a_w = N_KV_A * HEAD_DIM
    kv_b_w = N_KV_B * HEAD_DIM
    ga, gb = N_HEADS // N_KV_A, N_HEADS // N_KV_B
    rope = _rope_tables(seq)
    xs = x.reshape(batch * seq, d).astype(F32)
    kt_sh = v_sh = None
    wg_all, wu_all = _split_gate_up(w_gate_up)
    wd_all = w_down
    for layer in range(depth):
        if layer < n_a:
            q_pad, ka, va = _project(xs, w_qkv_a[layer].astype(BF16), jnp.zeros((attn_w + 2 * kv_a_w,), F32), rope,
                                     (attn_w, kv_a_w, kv_a_w), attn_w + kv_a_w, seq, tn=512,
                                     modes=('zero', 'block', 'one'))
            kmean = _kmean(ka, seq // MOBA_BLOCK)
            attn = _moba_attention(q_pad.reshape(batch, N_KV_A, ga, seq, LANES), ka.transpose(0, 1, 3, 2), va, kmean)
            attn2d = attn.reshape(batch * seq, attn_w)
            wo, bo = w_o_a[layer], jnp.zeros((d,), F32)
        else:
            jb = layer - n_a
            if jb == 0:
                k_pad, v_sh = _project(xs, w_kv_shared.astype(BF16), b_kv_shared, rope, (kv_b_w, kv_b_w), kv_b_w, seq,
                                       tn=256, modes=('zero', 'zero'))
                kt_sh = k_pad.transpose(0, 1, 3, 2)
            (q_pad,) = _project(xs, w_q_b[jb].astype(BF16), b_q_b[jb], rope, (attn_w,), attn_w, seq, tn=512,
                                modes=('zero',))
            attn = _swa_attention(q_pad.reshape(batch, N_KV_B, gb, seq, LANES), kt_sh, v_sh, sinks_b[jb])
            attn2d = attn.reshape(batch * seq, attn_w)
            wo, bo = w_o_b[jb], b_o_b[jb]
        x1, xp, top_idx, gates = _oproj_ln_router(attn2d, xs, wo.astype(BF16), bo, ln_mix_g[layer],
                                                  ln_mix_b[layer], w_router[layer], b_router[layer], alpha)
        xs = _moe_layer(x1, xp, top_idx, gates, layer, wg_all, wu_all, b_gate_up[layer], wd_all, b_down[layer],
                        ln_ffn_g[layer], ln_ffn_b[layer], alpha)
    return xs.reshape(batch, seq, d).astype(x.dtype)
```

```python
import functools

import numpy as np
import jax
import jax.numpy as jnp
from jax import lax
from jax.experimental import pallas as pl
from jax.experimental.pallas import tpu as pltpu

F32 = jnp.float32
BF16 = jnp.bfloat16
NEG_INF = float("-inf")

HEAD_DIM = 64
N_HEADS = 32
N_KV_A = 8
N_KV_B = 4
ROPE_DIM = 16
ROPE_THETA = 500000.0
MOBA_BLOCK = 256
MOBA_TOPK = 3
SWA_WINDOW = 128
N_EXPERTS = 32
TOP_K = 4
SWIGLU_LIMIT = 7.0
SWIGLU_ALPHA = 1.702
LN_EPS = 1e-5

MOBA_MASK_BIAS = -(2.0 ** 100)

LANES = 128
SUBLANES = 8
V7X_VMEM_BYTES = 64 * 1024 * 1024
VMEM_LIMIT = V7X_VMEM_BYTES - 8 * 1024 * 1024

MOBA_KV_TILE = 512
TM_PROJ = 512
TM_OPROJ = 512
TM_MOE = 512
TM_COMBINE = 256
HC_MOE = 256
TR_SPLIT = 1024


def _cparams(semantics):
    return pltpu.CompilerParams(dimension_semantics=semantics, vmem_limit_bytes=VMEM_LIMIT)


def _proj_kernel(x_ref, w_ref, b_ref, c_ref, sa_ref, sb_ref, *o_refs, splits, modes, rope_cols, tn, tm, pos_blocks):
    x = x_ref[...].astype(BF16)
    lane = lax.broadcasted_iota(jnp.int32, (tm, LANES), 1)
    lower = lane < HEAD_DIM
    fills = {}
    if 'zero' in modes:
        fills['zero'] = jnp.zeros((tm, LANES), F32)
    if 'one' in modes:
        fills['one'] = jnp.where(lane == HEAD_DIM, 1.0, 0.0)
    if 'block' in modes:
        pos = (pl.program_id(0) % pos_blocks) * tm + lax.broadcasted_iota(jnp.int32, (tm, LANES), 0)
        fills['block'] = jnp.where(lane - HEAD_DIM == pos // MOBA_BLOCK, 1.0, 0.0)
    col = 0
    for o_ref, width, mode in zip(o_refs, splits, modes):
        for c0 in range(0, width, tn):
            lo = col + c0
            y = jnp.dot(x, w_ref[:, lo:lo + tn], preferred_element_type=F32) + b_ref[:, lo:lo + tn]
            for l0 in range(0, tn, LANES):
                seg = y[:, l0:l0 + LANES]
                if lo < rope_cols:
                    seg = (seg * c_ref[...] + pltpu.roll(seg, LANES - ROPE_DIM // 2, 1) * sa_ref[...]
                           + pltpu.roll(seg, ROPE_DIM // 2, 1) * sb_ref[...])
                if mode is None:
                    o_ref[:, c0 + l0:c0 + l0 + LANES] = seg.astype(o_ref.dtype)
                else:
                    head = (c0 + l0) // HEAD_DIM
                    o_ref[head] = jnp.where(lower, seg, fills[mode]).astype(o_ref.dtype)
                    o_ref[head + 1] = jnp.where(lower, pltpu.roll(seg, HEAD_DIM, 1), fills[mode]).astype(o_ref.dtype)
        col += width


def _project(x2d, w_bf16, bias, rope_tabs, splits, rope_cols, seq, tn, modes=None):
    n, d = x2d.shape
    nout = w_bf16.shape[1]
    tm = min(TM_PROJ, seq)
    modes = tuple(modes) if modes is not None else (None,) * len(splits)
    assert n % tm == 0 and seq % tm == 0 and sum(splits) == nout
    assert all(s % tn == 0 for s in splits) and rope_cols % tn == 0 and tn % LANES == 0
    assert 2 * HEAD_DIM == LANES and (tm % MOBA_BLOCK == 0 or 'block' not in modes)
    pos_blocks = seq // tm
    batch = n // seq
    c_tab, sa_tab, sb_tab = rope_tabs
    tab_spec = pl.BlockSpec((tm, LANES), lambda i: (i % pos_blocks, 0))
    out_specs, out_shape = [], []
    for s, mode in zip(splits, modes):
        if mode is None:
            out_specs.append(pl.BlockSpec((tm, s), lambda i: (i, 0)))
            out_shape.append(jax.ShapeDtypeStruct((n, s), BF16))
        else:
            heads = s // HEAD_DIM
            out_specs.append(pl.BlockSpec((None, heads, tm, LANES), lambda i: (i // pos_blocks, 0, i % pos_blocks, 0)))
            out_shape.append(jax.ShapeDtypeStruct((batch, heads, seq, LANES), BF16))
    kern = functools.partial(_proj_kernel, splits=tuple(splits), modes=modes, rope_cols=rope_cols, tn=tn, tm=tm,
                             pos_blocks=pos_blocks)
    return pl.pallas_call(
        kern,
        grid=(n // tm,),
        in_specs=[
            pl.BlockSpec((tm, d), lambda i: (i, 0)),
            pl.BlockSpec((d, nout), lambda i: (0, 0)),
            pl.BlockSpec((1, nout), lambda i: (0, 0)),
            tab_spec, tab_spec, tab_spec,
        ],
        out_specs=out_specs,
        out_shape=out_shape,
        compiler_params=_cparams(("parallel",)),
        name="proj_rope",
    )(x2d, w_bf16, bias.reshape(1, nout).astype(F32), c_tab, sa_tab, sb_tab)


def _rope_tables(seq):
    half = ROPE_DIM // 2
    inv = ROPE_THETA ** (-jnp.arange(0, ROPE_DIM, 2, dtype=F32) / ROPE_DIM)
    ang = jnp.arange(seq, dtype=F32)[:, None] * inv[None, :]
    cos, sin = jnp.cos(ang), jnp.sin(ang)
    ones = jnp.ones((seq, HEAD_DIM - ROPE_DIM), F32)
    zeros = jnp.zeros((seq, HEAD_DIM - ROPE_DIM), F32)
    zh = jnp.zeros((seq, half), F32)
    c_head = jnp.concatenate([cos, cos, ones], axis=1)
    sa_head = jnp.concatenate([-sin, zh, zeros], axis=1)
    sb_head = jnp.concatenate([zh, sin, zeros], axis=1)
    rep = LANES // HEAD_DIM
    return tuple(jnp.tile(t, (1, rep)) for t in (c_head, sa_head, sb_head))


def _kmean_kernel(k_ref, o_ref, *, nb, blk):
    k = k_ref[...].astype(F32)
    km = jnp.mean(k.reshape(nb, blk, LANES), axis=1)
    lane = lax.broadcasted_iota(jnp.int32, (nb, LANES), 1)
    o_ref[...] = jnp.zeros(o_ref.shape, F32)
    o_ref[0:nb, :] = jnp.where(lane < HEAD_DIM, km, 0.0)


def _kmean(ka, nb):
    b, hkv, s, _ = ka.shape
    assert nb <= LANES
    return pl.pallas_call(
        functools.partial(_kmean_kernel, nb=nb, blk=MOBA_BLOCK),
        grid=(b, hkv),
        in_specs=[pl.BlockSpec((None, None, s, LANES), lambda i, j: (i, j, 0, 0))],
        out_specs=pl.BlockSpec((None, None, LANES, LANES), lambda i, j: (i, j, 0, 0)),
        out_shape=jax.ShapeDtypeStruct((b, hkv, LANES, LANES), F32),
        compiler_params=_cparams(("parallel", "parallel")),
        name="moba_kmean",
    )(ka)


def _moba_kernel(it_ref, jt_ref, q_ref, kta_ref, va_ref, km_ref, o_ref, qa_scr, m_scr, acc_scr,
                 *, groups, blk, kv_tile, nbp, scale):
    p = pl.program_id(2)
    i = it_ref[p]
    jt = jt_ref[p]
    own_tile = i // (kv_tile // blk)
    is_diag = jt == own_tile
    rows = groups * blk

    def softmax_step(s, first):
        parts = [s[:, c:c + LANES] for c in range(0, kv_tile, LANES)]
        mx = parts[0]
        for part in parts[1:]:
            mx = jnp.maximum(mx, part)
        rm = jnp.max(mx, axis=-1, keepdims=True)
        if first:
            m_new = jnp.broadcast_to(rm, (rows, LANES))
        else:
            m_old = m_scr[...]
            m_new = jnp.maximum(m_old, rm)
        pr = jnp.concatenate([jnp.exp(part - m_new) for part in parts], axis=1).astype(BF16)
        pv = jnp.dot(pr, va_ref[...], preferred_element_type=F32)
        if first:
            acc_scr[...] = pv
        else:
            acc_scr[...] = jnp.exp(m_old - m_new) * acc_scr[...] + pv
        m_scr[...] = m_new

    @pl.when(is_diag)
    def _():
        qb = q_ref[...].reshape(rows, LANES)
        qf = qb.astype(F32)
        km = km_ref[0:nbp, :]
        km_hi = km.astype(BF16)
        km_lo = (km - km_hi.astype(F32)).astype(BF16)
        nt_dims = (((1,), (1,)), ((), ()))
        gate = (lax.dot_general(km_hi, qb, nt_dims, preferred_element_type=F32)
                + lax.dot_general(km_lo, qb, nt_dims, preferred_element_type=F32))
        kb = lax.broadcasted_iota(jnp.int32, (nbp, rows), 0)
        kb_f = kb.astype(F32)
        cur = jnp.where(kb < i, gate, NEG_INF)
        sel = jnp.zeros((nbp, rows), F32)
        for _ in range(MOBA_TOPK):
            mx = jnp.max(cur, axis=0, keepdims=True)
            idx = jnp.min(jnp.where(cur == mx, kb_f, float(nbp)), axis=0, keepdims=True)
            hit = kb_f == idx
            sel = jnp.where(hit & (mx > NEG_INF), 1.0, sel)
            cur = jnp.where(hit, NEG_INF, cur)
        bias_t = jnp.where((sel > 0.0) | (kb >= i), 0.0, MOBA_MASK_BIAS)
        bias_t = jnp.concatenate([jnp.zeros((HEAD_DIM, rows), F32), bias_t,
                                  jnp.zeros((LANES - HEAD_DIM - nbp, rows), F32)], axis=0)
        qa = (qf * scale + bias_t.T).astype(BF16)
        qa_scr[...] = qa
        s = jnp.dot(qa, kta_ref[...], preferred_element_type=F32)
        qpos = i * blk + lax.broadcasted_iota(jnp.int32, (groups, blk, kv_tile), 1).reshape(rows, kv_tile)
        kpos = jt * kv_tile + lax.broadcasted_iota(jnp.int32, (rows, kv_tile), 1)
        s = jnp.where(kpos <= qpos, s, NEG_INF)
        softmax_step(s, True)

    @pl.when(jnp.logical_not(is_diag))
    def _():
        s = jnp.dot(qa_scr[...], kta_ref[...], preferred_element_type=F32)
        softmax_step(s, False)

    @pl.when((jt == own_tile - 1) | (own_tile == 0))
    def _():
        acc = acc_scr[...]
        o = acc / acc[:, HEAD_DIM:HEAD_DIM + 1]
        lower = lax.broadcasted_iota(jnp.int32, (blk, LANES), 1) < HEAD_DIM
        pairs = [jnp.where(lower, o[g * blk:(g + 1) * blk], pltpu.roll(o[(g + 1) * blk:(g + 2) * blk], HEAD_DIM, 1))
                 for g in range(0, groups, 2)]
        o_ref[...] = jnp.concatenate(pairs, axis=1).astype(o_ref.dtype)


def _moba_attention(q_pad, kta, va, kmean):
    b, hkv, groups, s, _ = q_pad.shape
    blk = MOBA_BLOCK
    assert s % blk == 0 and blk == 2 * LANES
    nb = s // blk
    assert nb <= LANES - HEAD_DIM
    kv_tile = MOBA_KV_TILE if s % MOBA_KV_TILE == 0 else blk
    per_tile = kv_tile // blk
    it, jt = [], []
    for i in range(nb):
        it.append(i)
        jt.append(i // per_tile)
        for j in range(i // per_tile):
            it.append(i)
            jt.append(j)
    it = jnp.asarray(np.asarray(it, np.int32))
    jt = jnp.asarray(np.asarray(jt, np.int32))
    nbp = -(-nb // SUBLANES) * SUBLANES
    assert nbp < LANES - HEAD_DIM
    kern = functools.partial(_moba_kernel, groups=groups, blk=blk, kv_tile=kv_tile, nbp=nbp,
                             scale=HEAD_DIM ** -0.5)
    grid_spec = pltpu.PrefetchScalarGridSpec(
        num_scalar_prefetch=2,
        grid=(b, hkv, int(it.shape[0])),
        in_specs=[
            pl.BlockSpec((None, None, groups, blk, LANES), lambda bi, h, p, it, jt: (bi, h, 0, it[p], 0)),
            pl.BlockSpec((None, None, LANES, kv_tile), lambda bi, h, p, it, jt: (bi, h, 0, jt[p])),
            pl.BlockSpec((None, None, kv_tile, LANES), lambda bi, h, p, it, jt: (bi, h, jt[p], 0)),
            pl.BlockSpec((None, None, LANES, LANES), lambda bi, h, p, it, jt: (bi, h, 0, 0)),
        ],
        out_specs=pl.BlockSpec((None, blk, groups * HEAD_DIM), lambda bi, h, p, it, jt: (bi, it[p], h)),
        scratch_shapes=[
            pltpu.VMEM((groups * blk, LANES), BF16),
            pltpu.VMEM((groups * blk, LANES), F32),
            pltpu.VMEM((groups * blk, LANES), F32),
        ],
    )
    return pl.pallas_call(
        kern,
        grid_spec=grid_spec,
        out_shape=jax.ShapeDtypeStruct((b, s, hkv * groups * HEAD_DIM), BF16),
        compiler_params=_cparams(("parallel", "parallel", "arbitrary")),
        name="moba_attn",
    )(it, jt, q_pad, kta, va, kmean)


def _swa_kernel(q_ref, ktp_ref, ktc_ref, vp_ref, vc_ref, sink_ref, o_ref, *, groups, win, scale):
    kh = pl.program_id(1)
    i = pl.program_id(2)
    row = lax.broadcasted_iota(jnp.int32, (win, win), 0)
    col = lax.broadcasted_iota(jnp.int32, (win, win), 1)
    cur_mask = col <= row
    prev_mask = (col > row) & (i > 0)
    outs = []
    for g in range(groups):
        qg = q_ref[g]
        sp = jnp.dot(qg, ktp_ref[...], preferred_element_type=F32) * scale
        sc = jnp.dot(qg, ktc_ref[...], preferred_element_type=F32) * scale
        sp = jnp.where(prev_mask, sp, NEG_INF)
        sc = jnp.where(cur_mask, sc, NEG_INF)
        sink = sink_ref[kh * groups + g]
        mx = jnp.maximum(jnp.maximum(jnp.max(sp, axis=-1, keepdims=True), jnp.max(sc, axis=-1, keepdims=True)), sink)
        pp = jnp.exp(sp - mx)
        pc = jnp.exp(sc - mx)
        den = jnp.sum(pp, axis=-1, keepdims=True) + jnp.sum(pc, axis=-1, keepdims=True) + jnp.exp(sink - mx)
        o = (jnp.dot(pp.astype(BF16), vp_ref[...], preferred_element_type=F32)
             + jnp.dot(pc.astype(BF16), vc_ref[...], preferred_element_type=F32))
        outs.append(o / den)
    lower = lax.broadcasted_iota(jnp.int32, (win, LANES), 1) < HEAD_DIM
    pairs = [jnp.where(lower, outs[g], pltpu.roll(outs[g + 1], HEAD_DIM, 1)) for g in range(0, groups, 2)]
    o_ref[...] = jnp.concatenate(pairs, axis=1).astype(o_ref.dtype)


def _swa_attention(q_pad, kt_pad, v_pad, sinks):
    b, hkv, groups, s, _ = q_pad.shape
    win = SWA_WINDOW
    assert s % win == 0 and groups % 2 == 0
    nb = s // win
    kern = functools.partial(_swa_kernel, groups=groups, win=win, scale=HEAD_DIM ** -0.5)
    return pl.pallas_call(
        kern,
        grid=(b, hkv, nb),
        in_specs=[
            pl.BlockSpec((None, None, groups, win, LANES), lambda bi, h, i: (bi, h, 0, i, 0)),
            pl.BlockSpec((None, None, LANES, win), lambda bi, h, i: (bi, h, 0, jnp.maximum(i - 1, 0))),
            pl.BlockSpec((None, None, LANES, win), lambda bi, h, i: (bi, h, 0, i)),
            pl.BlockSpec((None, None, win, LANES), lambda bi, h, i: (bi, h, jnp.maximum(i - 1, 0), 0)),
            pl.BlockSpec((None, None, win, LANES), lambda bi, h, i: (bi, h, i, 0)),
            pl.BlockSpec(memory_space=pltpu.SMEM),
        ],
        out_specs=pl.BlockSpec((None, win, groups * HEAD_DIM), lambda bi, h, i: (bi, i, h)),
        out_shape=jax.ShapeDtypeStruct((b, s, hkv * groups * HEAD_DIM), BF16),
        compiler_params=_cparams(("parallel", "parallel", "parallel")),
        name="swa_attn",
    )(q_pad, kt_pad, kt_pad, v_pad, v_pad, sinks.astype(F32))


def _layer_norm_rows(z, g, b):
    mu = jnp.mean(z, axis=-1, keepdims=True)
    zc = z - mu
    var = jnp.mean(zc * zc, axis=-1, keepdims=True)
    return zc * lax.rsqrt(var + LN_EPS) * g + b


def _pack_bf16_pairs(v):
    half = v.shape[1] // 2
    hi = lax.bitcast_convert_type(v[:, :half].astype(BF16).astype(F32), jnp.uint32)
    lo = lax.bitcast_convert_type(v[:, half:].astype(BF16).astype(F32), jnp.uint32)
    return hi | (lo >> 16)


def _unpack_bf16_pairs(u):
    hi = lax.bitcast_convert_type(u & jnp.uint32(0xFFFF0000), F32)
    lo = lax.bitcast_convert_type(u << 16, F32)
    return hi, lo


def _oproj_kernel(a_ref, x_ref, wo_ref, bo_ref, g_ref, b_ref, wrh_ref, wrl_ref, br_ref, x1_ref, xp_ref, ti_ref, tg_ref,
                  *, alpha, n_experts):
    mix = jnp.dot(a_ref[...], wo_ref[...], preferred_element_type=F32) + bo_ref[...]
    x1 = _layer_norm_rows(alpha * x_ref[...] + mix, g_ref[...], b_ref[...])
    x1_ref[...] = x1
    xp_ref[...] = _pack_bf16_pairs(x1)
    xh = x1.astype(BF16)
    xl = (x1 - xh.astype(F32)).astype(BF16)
    logits = (jnp.dot(xh, wrh_ref[...], preferred_element_type=F32)
              + jnp.dot(xh, wrl_ref[...], preferred_element_type=F32)
              + jnp.dot(xl, wrh_ref[...], preferred_element_type=F32)) + br_ref[...]
    lane = lax.broadcasted_iota(jnp.int32, logits.shape, 1)
    cur = jnp.where(lane < n_experts, logits, NEG_INF)
    vals, idxs = [], []
    for _ in range(TOP_K):
        mx = jnp.max(cur, axis=-1, keepdims=True)
        idx = jnp.min(jnp.where(cur == mx, lane, LANES), axis=-1, keepdims=True)
        vals.append(mx)
        idxs.append(idx)
        cur = jnp.where(lane == idx, NEG_INF, cur)
    exps = [jnp.exp(v - vals[0]) for v in vals]
    den = exps[0]
    for e in exps[1:]:
        den = den + e
    ti = jnp.zeros(logits.shape, jnp.int32)
    tg = jnp.zeros(logits.shape, F32)
    for k in range(TOP_K):
        ti = jnp.where(lane == k, idxs[k], ti)
        tg = jnp.where(lane == k, exps[k] / den, tg)
    ti_ref[...] = ti
    tg_ref[...] = tg


def _oproj_ln_router(attn2d, x2d, wo_bf16, bo, ln_g, ln_b, w_r, b_r, alpha):
    n, d = x2d.shape
    da = attn2d.shape[1]
    tm = min(TM_OPROJ, n)
    assert n % tm == 0
    n_experts = w_r.shape[1]
    wr_pad = jnp.zeros((d, LANES), F32).at[:, :n_experts].set(w_r.astype(F32))
    wr_hi = wr_pad.astype(BF16)
    wr_lo = (wr_pad - wr_hi.astype(F32)).astype(BF16)
    br_pad = jnp.zeros((1, LANES), F32).at[0, :n_experts].set(b_r.astype(F32))
    row = lambda i: (i, 0)
    fixed = lambda i: (0, 0)
    return pl.pallas_call(
        functools.partial(_oproj_kernel, alpha=alpha, n_experts=n_experts),
        grid=(n // tm,),
        in_specs=[
            pl.BlockSpec((tm, da), row),
            pl.BlockSpec((tm, d), row),
            pl.BlockSpec((da, d), fixed),
            pl.BlockSpec((1, d), fixed),
            pl.BlockSpec((1, d), fixed),
            pl.BlockSpec((1, d), fixed),
            pl.BlockSpec((d, LANES), fixed),
            pl.BlockSpec((d, LANES), fixed),
            pl.BlockSpec((1, LANES), fixed),
        ],
        out_specs=[pl.BlockSpec((tm, d), row), pl.BlockSpec((tm, d // 2), row), pl.BlockSpec((tm, LANES), row),
                   pl.BlockSpec((tm, LANES), row)],
        out_shape=[jax.ShapeDtypeStruct((n, d), F32), jax.ShapeDtypeStruct((n, d // 2), jnp.uint32),
                   jax.ShapeDtypeStruct((n, LANES), jnp.int32), jax.ShapeDtypeStruct((n, LANES), F32)],
        compiler_params=_cparams(("parallel",)),
        name="oproj_ln_router",
    )(attn2d, x2d, wo_bf16, bo.reshape(1, d).astype(F32), ln_g.reshape(1, d).astype(F32),
      ln_b.reshape(1, d).astype(F32), wr_hi, wr_lo, br_pad)


def _split_gate_up_kernel(w_ref, pg_ref, pu_ref, g_ref, u_ref, *, width):
    for c in range(0, width, 2 * LANES):
        blk = w_ref[:, c:c + 2 * LANES].astype(BF16)
        g_ref[:, c // 2:c // 2 + LANES] = jnp.dot(blk, pg_ref[...], preferred_element_type=F32).astype(BF16)
        u_ref[:, c // 2:c // 2 + LANES] = jnp.dot(blk, pu_ref[...], preferred_element_type=F32).astype(BF16)


def _split_gate_up(w_gate_up):
    nl, ne, d, width = w_gate_up.shape
    assert width % (2 * LANES) == 0
    tr = min(TR_SPLIT, d)
    assert d % tr == 0
    k = jnp.arange(2 * LANES, dtype=jnp.int32)[:, None]
    j = jnp.arange(LANES, dtype=jnp.int32)[None, :]
    pick_gate = (k == 2 * j).astype(BF16)
    pick_lin = (k == 2 * j + 1).astype(BF16)
    out = jax.ShapeDtypeStruct((nl * ne, d, width // 2), BF16)
    return pl.pallas_call(
        functools.partial(_split_gate_up_kernel, width=width),
        grid=(nl * ne, d // tr),
        in_specs=[
            pl.BlockSpec((None, tr, width), lambda e, r: (e, r, 0)),
            pl.BlockSpec((2 * LANES, LANES), lambda e, r: (0, 0)),
            pl.BlockSpec((2 * LANES, LANES), lambda e, r: (0, 0)),
        ],
        out_specs=[pl.BlockSpec((None, tr, width // 2), lambda e, r: (e, r, 0))] * 2,
        out_shape=[out, out],
        compiler_params=_cparams(("parallel", "parallel")),
        name="split_gate_up",
    )(w_gate_up.reshape(nl * ne, d, width), pick_gate, pick_lin)


def _moe_gemm_kernel(be_ref, nu_ref, src_hbm, dst_hbm, x_hbm, wg_ref, wu_ref, bg_ref, bu_ref, wd_ref, bd_ref, y_hbm,
                     gtab, stab, xg0, xg1, yb0, yb1, yacc, gtsem, stsem, gsem, ssem, *, tm, hidden, hc, n_blocks):
    del be_ref
    i = pl.program_id(0)
    n_used = nu_ref[0]
    last = n_used - 1
    xg = (xg0, xg1)
    yb = (yb0, yb1)

    def src_copy(blk, s):
        return pltpu.make_async_copy(src_hbm.at[blk], gtab.at[pl.ds(s, 1)], gtsem.at[s])

    def dst_copy(blk, s):
        return pltpu.make_async_copy(dst_hbm.at[blk], stab.at[pl.ds(s, 1)], stsem.at[s])

    def gather_start(s, r, priority=0):
        tok = gtab[s, r]
        pltpu.make_async_copy(x_hbm.at[pl.ds(tok, 1)], xg[s].at[pl.ds(r, 1)], gsem.at[s]).start(priority)

    def gather_wait(s):
        pltpu.make_async_copy(x_hbm.at[pl.ds(0, tm)], xg[s], gsem.at[s]).wait()

    def scatter_start(s, r, priority=0):
        row = stab[s, r]
        pltpu.make_async_copy(yb[s].at[pl.ds(r, 1)], y_hbm.at[pl.ds(row, 1)], ssem.at[s]).start(priority)

    def scatter_wait(s):
        pltpu.make_async_copy(yb[s], y_hbm.at[pl.ds(0, tm)], ssem.at[s]).wait()

    @pl.when(i == 0)
    def _():
        src_copy(0, 0).start()
        src_copy(jnp.minimum(1, last), 1).start()
        dst_copy(n_blocks, 1).start()
        src_copy(0, 0).wait()

        def body(r, carry):
            gather_start(0, r)
            return carry
        lax.fori_loop(0, tm, body, 0, unroll=8)
        yb1[...] = jnp.zeros(yb1.shape, yb1.dtype)
        tail = y_hbm.shape[0] - 2 * tm
        for half in range(2):
            pltpu.make_async_copy(yb1, y_hbm.at[pl.ds(tail + half * tm, tm)], ssem.at[1]).start()
        for half in range(2):
            pltpu.make_async_copy(yb1, y_hbm.at[pl.ds(tail + half * tm, tm)], ssem.at[1]).wait()

    def step(slot):
        nslot = 1 - slot
        src_copy(jnp.minimum(i + 2, last), slot).start()
        dst_copy(i, slot).start()
        src_copy(0, nslot).wait()
        dst_copy(0, nslot).wait()
        gather_wait(slot)

        @pl.when(i >= 1)
        def _():
            scatter_wait(slot)

        x_hi, x_lo = _unpack_bf16_pairs(xg[slot][...])
        xb = jnp.concatenate([x_hi.astype(BF16), x_lo.astype(BF16)], axis=1)
        n_chunks = hidden // hc
        rows_per_chunk = tm // n_chunks
        for ci in range(n_chunks):
            c = ci * hc
            r0 = ci * rows_per_chunk

            @pl.when(n_used > 0)
            def _():
                for r in range(r0, r0 + rows_per_chunk):
                    gather_start(nslot, r, priority=r % 2)

            glu = jnp.dot(xb, wg_ref[:, c:c + hc], preferred_element_type=F32) + bg_ref[:, c:c + hc]
            lin = jnp.dot(xb, wu_ref[:, c:c + hc], preferred_element_type=F32) + bu_ref[:, c:c + hc]
            glu = jnp.minimum(glu, SWIGLU_LIMIT)
            lin = jnp.clip(lin, -SWIGLU_LIMIT, SWIGLU_LIMIT)
            act = glu * jax.nn.sigmoid(SWIGLU_ALPHA * glu) * (lin + 1.0)

            @pl.when(n_used > 0)
            def _():
                for r in range(r0, r0 + rows_per_chunk):
                    scatter_start(nslot, r, priority=r % 2)

            part = jnp.dot(act.astype(BF16), wd_ref[c:c + hc, :].astype(BF16), preferred_element_type=F32)
            if ci == 0:
                yacc[...] = part + bd_ref[...]
            else:
                yacc[...] += part
        yb[slot][...] = _pack_bf16_pairs(yacc[...])

    def drain(slot):
        nslot = 1 - slot
        src_copy(0, nslot).wait()
        dst_copy(0, nslot).wait()

        def body(r, carry):
            scatter_start(nslot, r)
            return carry
        lax.fori_loop(0, tm, body, 0, unroll=8)
        scatter_wait(slot)
        scatter_wait(nslot)
        gather_wait(slot)

    for parity in range(2):
        @pl.when((i < n_used) & (i % 2 == parity))
        def _():
            step(parity)

        @pl.when((i == n_used) & (i % 2 == parity))
        def _():
            drain(parity)


def _moe_gemm(xp, src_tab, dst_tab, block_expert, n_used, layer, wg, wu, bg, bu, wd, bd, tm):
    n, dp = xp.shape
    d = 2 * dp
    n_blocks = src_tab.shape[0]
    n_experts, hidden = wd.shape[1], wd.shape[2]
    hc = min(HC_MOE, hidden)
    assert hidden % hc == 0 and hc % LANES == 0 and tm % (hidden // hc) == 0
    assert dst_tab.shape[0] == n_blocks + 1 and block_expert.shape[0] == n_blocks + 1
    w_flat = lambda i, be, nu: (layer * n_experts + be[i], 0, 0)
    w_in = lambda i, be, nu: (be[i], 0, 0)
    grid_spec = pltpu.PrefetchScalarGridSpec(
        num_scalar_prefetch=2,
        grid=(n_blocks + 1,),
        in_specs=[
            pl.BlockSpec(memory_space=pl.ANY),
            pl.BlockSpec(memory_space=pl.ANY),
            pl.BlockSpec(memory_space=pl.ANY),
            pl.BlockSpec((None, d, hidden), w_flat),
            pl.BlockSpec((None, d, hidden), w_flat),
            pl.BlockSpec((None, 1, hidden), w_in),
            pl.BlockSpec((None, 1, hidden), w_in),
            pl.BlockSpec((None, None, hidden, d), lambda i, be, nu: (layer, be[i], 0, 0)),
            pl.BlockSpec((None, 1, d), w_in),
        ],
        out_specs=pl.BlockSpec(memory_space=pl.ANY),
        scratch_shapes=[
            pltpu.SMEM((2, tm), jnp.int32),
            pltpu.SMEM((2, tm), jnp.int32),
            pltpu.VMEM((tm, dp), jnp.uint32),
            pltpu.VMEM((tm, dp), jnp.uint32),
            pltpu.VMEM((tm, dp), jnp.uint32),
            pltpu.VMEM((tm, dp), jnp.uint32),
            pltpu.VMEM((tm, d), F32),
            pltpu.SemaphoreType.DMA((2,)),
            pltpu.SemaphoreType.DMA((2,)),
            pltpu.SemaphoreType.DMA((2,)),
            pltpu.SemaphoreType.DMA((2,)),
        ],
    )
    y_rows = n * TOP_K + 2 * tm
    return pl.pallas_call(
        functools.partial(_moe_gemm_kernel, tm=tm, hidden=hidden, hc=hc, n_blocks=n_blocks),
        grid_spec=grid_spec,
        out_shape=jax.ShapeDtypeStruct((y_rows, dp), jnp.uint32),
        compiler_params=_cparams(("arbitrary",)),
        name="moe_gemm",
    )(block_expert, n_used, src_tab, dst_tab, xp, wg, wu, bg, bu, wd, bd)


def _dispatch_tables(top_idx, tm, n_blocks):
    n = top_idx.shape[0]
    nk = n * TOP_K
    e_flat = top_idx.reshape(-1)
    onehot = (e_flat[:, None] == jnp.arange(N_EXPERTS, dtype=jnp.int32)[None, :]).astype(jnp.int32)
    csum = jnp.cumsum(onehot, axis=0)
    counts = csum[-1]
    rank = jnp.sum((csum - onehot) * onehot, axis=1)
    blocks_e = (counts + tm - 1) // tm
    bend = jnp.cumsum(blocks_e)
    bstart = bend - blocks_e
    dest = jnp.sum(onehot * bstart[None, :], axis=1) * tm + rank
    row_pair = jnp.full((n_blocks * tm,), -1, jnp.int32).at[dest].set(jnp.arange(nk, dtype=jnp.int32))
    row_pair = row_pair.reshape(n_blocks, tm)
    blk_ids = jnp.arange(n_blocks + 1, dtype=jnp.int32)
    trash = nk + (blk_ids[:n_blocks, None] % 2) * tm + jnp.arange(tm, dtype=jnp.int32)[None, :]
    src_tab = jnp.where(row_pair >= 0, row_pair // TOP_K, 0).astype(jnp.int32).reshape(n_blocks, 1, tm)
    dst_tab = jnp.where(row_pair >= 0, (row_pair % TOP_K) * n + row_pair // TOP_K, trash)
    before_first = nk + tm + jnp.arange(tm, dtype=jnp.int32)[None, :]
    dst_tab = jnp.concatenate([dst_tab, before_first], axis=0).astype(jnp.int32).reshape(n_blocks + 1, 1, tm)
    block_expert = jnp.minimum(jnp.sum((bend[None, :] <= blk_ids[:, None]).astype(jnp.int32), axis=1),
                               N_EXPERTS - 1).astype(jnp.int32)
    n_used = bend[-1:].astype(jnp.int32)
    return src_tab, dst_tab, block_expert, n_used


def _combine_kernel(*refs, alpha):
    y_refs, (x_ref, tg_ref, g_ref, b_ref, o_ref) = refs[:TOP_K], refs[TOP_K:]
    tg = tg_ref[...]
    y = None
    for k in range(TOP_K):
        yk = jnp.concatenate(_unpack_bf16_pairs(y_refs[k][...]), axis=1)
        y = tg[:, k:k + 1] * yk if y is None else y + tg[:, k:k + 1] * yk
    o_ref[...] = _layer_norm_rows(alpha * x_ref[...] + y, g_ref[...], b_ref[...])


def _combine_ln(y4, x1, gates, ln_g, ln_b, alpha):
    n, d = x1.shape
    tm = min(TM_COMBINE, n)
    assert n % tm == 0
    row = lambda i: (i, 0)
    fixed = lambda i: (0, 0)
    return pl.pallas_call(
        functools.partial(_combine_kernel, alpha=alpha),
        grid=(n // tm,),
        in_specs=[pl.BlockSpec((tm, d // 2), functools.partial(lambda k, i: (k * (n // tm) + i, 0), k))
                  for k in range(TOP_K)] + [
            pl.BlockSpec((tm, d), row),
            pl.BlockSpec((tm, LANES), row),
            pl.BlockSpec((1, d), fixed),
            pl.BlockSpec((1, d), fixed),
        ],
        out_specs=pl.BlockSpec((tm, d), row),
        out_shape=jax.ShapeDtypeStruct((n, d), F32),
        compiler_params=_cparams(("parallel",)),
        name="moe_combine_ln",
    )(*([y4] * TOP_K), x1, gates, ln_g.reshape(1, d).astype(F32), ln_b.reshape(1, d).astype(F32))


def _moe_layer(x1, xp, top_idx, gates, layer, wg, wu, b_gate_up, wd, b_down, ln_g, ln_b, alpha):
    n, d = x1.shape
    tm = min(TM_MOE, n)
    n_blocks = -(-(n * TOP_K) // tm) + N_EXPERTS
    src_tab, dst_tab, block_expert, n_used = _dispatch_tables(top_idx[:, :TOP_K], tm, n_blocks)
    hidden = wd.shape[2]
    bg = b_gate_up[:, 0::2].reshape(N_EXPERTS, 1, hidden).astype(F32)
    bu = b_gate_up[:, 1::2].reshape(N_EXPERTS, 1, hidden).astype(F32)
    bd = b_down.reshape(N_EXPERTS, 1, d).astype(F32)
    y4 = _moe_gemm(xp, src_tab, dst_tab, block_expert, n_used, layer, wg, wu, bg, bu, wd, bd, tm)
    return _combine_ln(y4, x1, gates, ln_g, ln_b, alpha)


def kernel(x, w_qkv_a, w_o_a, w_q_b, b_q_b, sinks_b, w_o_b, b_o_b, w_kv_shared, b_kv_shared, ln_mix_g, ln_mix_b, ln_ffn_g, ln_ffn_b, w_router, b_router, w_gate_up, b_gate_up, w_down, b_down):
    batch, seq, d = x.shape
    depth = ln_mix_g.shape[0]
    n_a = w_qkv_a.shape[0]
    alpha = (2 * depth) ** 0.25
    attn_w = N_HEADS * HEAD_DIM
    kv_a_w = N_KV_A * HEAD_DIM
    kv_b_w = N_KV_B * HEAD_DIM
    ga, gb = N_HEADS // N_KV_A, N_HEADS // N_KV_B
    rope = _rope_tables(seq)
    xs = x.reshape(batch * seq, d).astype(F32)
    kt_sh = v_sh = None
    wg_all, wu_all = _split_gate_up(w_gate_up)
    wd_all = w_down
    for layer in range(depth):
        if layer < n_a:
            q_pad, ka, va = _project(xs, w_qkv_a[layer].astype(BF16), jnp.zeros((attn_w + 2 * kv_a_w,), F32), rope,
                                     (attn_w, kv_a_w, kv_a_w), attn_w + kv_a_w, seq, tn=512,
                                     modes=('zero', 'block', 'one'))
            kmean = _kmean(ka, seq // MOBA_BLOCK)
            attn = _moba_attention(q_pad.reshape(batch, N_KV_A, ga, seq, LANES), ka.transpose(0, 1, 3, 2), va, kmean)
            attn2d = attn.reshape(batch * seq, attn_w)
            wo, bo = w_o_a[layer], jnp.zeros((d,), F32)
        else:
            jb = layer - n_a
            if jb == 0:
                k_pad, v_sh = _project(xs, w_kv_shared.astype(BF16), b_kv_shared, rope, (kv_b_w, kv_b_w), kv_b_w, seq,
                                       tn=256, modes=('zero', 'zero'))
                kt_sh = k_pad.transpose(0, 1, 3, 2)
            (q_pad,) = _project(xs, w_q_b[jb].astype(BF16), b_q_b[jb], rope, (attn_w,), attn_w, seq, tn=512,
                                modes=('zero',))
            attn = _swa_attention(q_pad.reshape(batch, N_KV_B, gb, seq, LANES), kt_sh, v_sh, sinks_b[jb])
            attn2d = attn.reshape(batch * seq, attn_w)
            wo, bo = w_o_b[jb], b_o_b[jb]
        x1, xp, top_idx, gates = _oproj_ln_router(attn2d, xs, wo.astype(BF16), bo, ln_mix_g[layer],
                                                  ln_mix_b[layer], w_router[layer], b_router[layer], alpha)
        xs = _moe_layer(x1, xp, top_idx, gates, layer, wg_all, wu_all, b_gate_up[layer], wd_all, b_down[layer],
                        ln_ffn_g[layer], ln_ffn_b[layer], alpha)
    return xs.reshape(batch, seq, d).astype(x.dtype)
```
